```python
import functools
import jax, jax.numpy as jnp
from jax import lax
import numpy as np

D_MODEL = 1024
BATCH = 32
SEQ = 256
DEPTH = 2
DEC_BATCH = 4
DEC_SEQ = 4096
PAST_LEN = 512

GRID_W = 64
WIN_ROWS = 8
WIN_COLS = 16
N_HEADS = 8
HEAD_DIM = 64
ATTN_DIM = N_HEADS * HEAD_DIM
POOL_GROUPS = 4
POOL_GROUP_DIM = 64
POOL_DIM = POOL_GROUPS * POOL_GROUP_DIM
POOL_WINDOWS = (2, 4, 8, 16)
CONV_DIM = 256
CONV_WIDTH = 3
N_BRANCH = 3
FFN_DIM = 2816
N_MOD = 9
Q_BLOCK = 128
EPS = 1e-6
IN_OFFSETS = (ATTN_DIM, 2 * ATTN_DIM, 3 * ATTN_DIM,
              3 * ATTN_DIM + POOL_DIM,
              3 * ATTN_DIM + POOL_DIM + CONV_DIM,
              3 * ATTN_DIM + POOL_DIM + 2 * CONV_DIM,
              3 * ATTN_DIM + POOL_DIM + 3 * CONV_DIM)
IN_COLS = IN_OFFSETS[-1] + N_BRANCH * D_MODEL

kernel_name = "hybrid_diffusion_na_pool_conv_step"


def rmsnorm(x, g):
    x32 = x.astype(jnp.float32)
    y = x32 * lax.rsqrt(jnp.mean(x32 * x32, axis=-1, keepdims=True) + EPS)
    return y.astype(x.dtype) * g


def modulate(x, shift, scale):
    return x * (1 + scale) + shift


def adaln(cvec, w_mod, b_mod):
    m = jax.nn.silu(cvec) @ w_mod + b_mod
    return m.reshape(cvec.shape[0], N_MOD, D_MODEL)


def swiglu(h, w_gate, w_up, w_down):
    return (jax.nn.silu(h @ w_gate) * (h @ w_up)) @ w_down


def to_heads(t):
    B, L, _ = t.shape
    return t.reshape(B, L, N_HEADS, HEAD_DIM).transpose(0, 2, 1, 3)


def softmax_f32(s, dtype):
    return jax.nn.softmax(s.astype(jnp.float32), axis=-1).astype(dtype)


def context_attention(q, k, v):
    B, H, L, hd = q.shape
    nb = L // Q_BLOCK
    scale = HEAD_DIM ** -0.5
    qb = q.reshape(B, H, nb, Q_BLOCK, hd).transpose(2, 0, 1, 3, 4)

    def block(qi):
        s = jnp.einsum('bhqd,bhkd->bhqk', qi, k).astype(jnp.float32) * scale
        return jnp.einsum('bhqk,bhkd->bhqd', softmax_f32(s, v.dtype), v)

    o = lax.map(block, qb)
    return o.transpose(1, 2, 0, 3, 4).reshape(B, H, L, hd)


def neighbourhood_attention(q, k, v, k_ctx, v_ctx, rpb):
    B, H, T, hd = q.shape
    rows = T // GRID_W
    win_r = min(WIN_ROWS, rows)
    n_loc = win_r * WIN_COLS
    scale = HEAD_DIM ** -0.5
    kg = k.reshape(B, H, rows, GRID_W, hd)
    vg = v.reshape(B, H, rows, GRID_W, hd)
    q_rows = q.reshape(B, H, rows, GRID_W, hd).transpose(2, 0, 1, 3, 4)
    cols = jnp.arange(GRID_W)
    col_start = jnp.clip(cols - WIN_COLS // 2, 0, GRID_W - WIN_COLS)
    col_idx = col_start[:, None] + jnp.arange(WIN_COLS)[None, :]
    col_off = col_idx - cols[:, None] + (WIN_COLS - 1)
    rpb_cols = rpb[:, :, col_off]

    def row_block(args):
        r, q_r = args
        rs = jnp.clip(r - win_r // 2, 0, rows - win_r)
        kb = lax.dynamic_slice_in_dim(kg, rs, win_r, axis=2)[:, :, :, col_idx]
        vb = lax.dynamic_slice_in_dim(vg, rs, win_r, axis=2)[:, :, :, col_idx]
        row_off = rs + jnp.arange(win_r) - r + (WIN_ROWS - 1)
        bias = rpb_cols[:, row_off].transpose(0, 2, 1, 3).astype(jnp.float32)
        s_loc = jnp.einsum('bhqd,bhrqjd->bhqrj', q_r, kb).astype(jnp.float32) * scale + bias[None]
        s_ctx = jnp.einsum('bhqd,bhkd->bhqk', q_r, k_ctx).astype(jnp.float32) * scale
        s = jnp.concatenate([s_loc.reshape(B, H, GRID_W, n_loc), s_ctx], axis=-1)
        p = softmax_f32(s, v.dtype)
        p_loc = p[..., :n_loc].reshape(B, H, GRID_W, win_r, WIN_COLS)
        return (jnp.einsum('bhqrj,bhrqjd->bhqd', p_loc, vb)
                + jnp.einsum('bhqk,bhkd->bhqd', p[..., n_loc:], v_ctx))

    o = lax.map(row_block, (jnp.arange(rows), q_rows))
    return o.transpose(1, 2, 0, 3, 4).reshape(B, H, T, hd)


def multiscale_pool(u, w_pool, pool_scale):
    B, L, _ = u.shape
    t = jnp.arange(L)
    ug = u.reshape(B, L, POOL_GROUPS, POOL_GROUP_DIM)
    outs = []
    for g, w in enumerate(POOL_WINDOWS):
        x_g = ug[:, :, g].astype(jnp.float32)
        cs = jnp.concatenate([jnp.zeros((B, 1, POOL_GROUP_DIM), jnp.float32),
                              jnp.cumsum(x_g, axis=1)], axis=1)
        lo = jnp.clip(t - w // 2, 0, L)
        hi = jnp.clip(t - w // 2 + w, 0, L)
        mean = (jnp.take(cs, hi, axis=1) - jnp.take(cs, lo, axis=1)) / (hi - lo).astype(jnp.float32)[None, :, None]
        outs.append((mean - x_g).astype(u.dtype))
    d = jnp.stack(outs, axis=2)
    y = jnp.einsum('blgc,gcd->blgd', d, w_pool).reshape(B, L, POOL_DIM)
    return y * pool_scale


def short_conv(z, w_conv, b_conv):
    y = lax.conv_general_dilated(z, w_conv[:, None, :], window_strides=(1,),
                                 padding=((CONV_WIDTH // 2, CONV_WIDTH // 2),),
                                 dimension_numbers=('NWC', 'WIO', 'NWC'),
                                 feature_group_count=CONV_DIM)
    return y + b_conv


def trunk_layer(x, mod, lp, attend):
    m = [mod[:, i, None, :] for i in range(N_MOD)]
    B, L, _ = x.shape
    h = modulate(rmsnorm(x, lp['g_ffn1']), m[0], m[1])
    x = x + 0.5 * m[2] * swiglu(h, lp['w_ffn1_gate'], lp['w_ffn1_up'], lp['w_ffn1_down'])
    n = modulate(rmsnorm(x, lp['g_mix']), m[3], m[4])
    proj = n @ lp['w_in']
    q, k, v, u_pool, u_conv, gate_b, gate_c, merge = jnp.split(proj, IN_OFFSETS, axis=-1)
    q = rmsnorm(to_heads(q), lp['g_q'])
    k = rmsnorm(to_heads(k), lp['g_k'])
    v = to_heads(v)
    a = attend(q, k, v).transpose(0, 2, 1, 3).reshape(B, L, ATTN_DIM)
    pl = multiscale_pool(u_pool, lp['w_pool'], lp['pool_scale'])
    cv = gate_b * short_conv(gate_c * u_conv, lp['w_conv'], lp['b_conv'])
    g_a, g_p, g_c = jnp.split(jax.nn.sigmoid(merge), N_BRANCH, axis=-1)
    merged = (g_a * (a @ lp['w_br_attn']) + g_p * (pl @ lp['w_br_pool'])
              + g_c * (cv @ lp['w_br_conv']))
    x = x + m[5] * (merged @ lp['w_out'])
    h = modulate(rmsnorm(x, lp['g_ffn2']), m[6], m[7])
    x = x + 0.5 * m[8] * swiglu(h, lp['w_ffn2_gate'], lp['w_ffn2_up'], lp['w_ffn2_down'])
    return x, k, v


def setup_inputs(seed: int = 0) -> dict:
    key = jax.random.key(seed)
    ks = iter(jax.random.split(key, 40))
    f32 = jnp.float32

    def nrm(shape, s=1.0):
        return jax.random.normal(next(ks), shape, f32) * s

    def gain(shape):
        return 1.0 + nrm(shape, 0.05)

    D = D_MODEL
    return {
        'x_prompt': nrm((BATCH, SEQ, D)),
        'x_sample': nrm((DEC_BATCH, DEC_SEQ, D)),
        'cache_k': nrm((DEC_BATCH, DEPTH, N_HEADS, PAST_LEN, HEAD_DIM)),
        'cache_v': nrm((DEC_BATCH, DEPTH, N_HEADS, PAST_LEN, HEAD_DIM)),
        'c': nrm((DEC_BATCH, D)),
        'c_ctx': nrm((D,)),
        'w_mod': nrm((DEPTH, D, N_MOD * D), 0.5 * D ** -0.5),
        'b_mod': nrm((DEPTH, N_MOD * D), 0.02),
        'g_ffn1': gain((DEPTH, D)),
        'w_ffn1_gate': nrm((DEPTH, D, FFN_DIM), D ** -0.5),
        'w_ffn1_up': nrm((DEPTH, D, FFN_DIM), D ** -0.5),
        'w_ffn1_down': nrm((DEPTH, FFN_DIM, D), FFN_DIM ** -0.5),
        'g_mix': gain((DEPTH, D)),
        'w_in': nrm((DEPTH, D, IN_COLS), D ** -0.5),
        'g_q': gain((DEPTH, HEAD_DIM)),
        'g_k': gain((DEPTH, HEAD_DIM)),
        'rpb': nrm((DEPTH, N_HEADS, 2 * WIN_ROWS - 1, 2 * WIN_COLS - 1), 0.5),
        'w_pool': nrm((DEPTH, POOL_GROUPS, POOL_GROUP_DIM, POOL_GROUP_DIM), POOL_GROUP_DIM ** -0.5),
        'pool_scale': 1.0 + nrm((DEPTH, POOL_DIM), 0.1),
        'w_conv': nrm((DEPTH, CONV_WIDTH, CONV_DIM), CONV_WIDTH ** -0.5),
        'b_conv': nrm((DEPTH, CONV_DIM), 0.02),
        'w_br_attn': nrm((DEPTH, ATTN_DIM, D), ATTN_DIM ** -0.5),
        'w_br_pool': nrm((DEPTH, POOL_DIM, D), POOL_DIM ** -0.5),
        'w_br_conv': nrm((DEPTH, CONV_DIM, D), CONV_DIM ** -0.5),
        'w_out': nrm((DEPTH, D, D), D ** -0.5),
        'g_ffn2': gain((DEPTH, D)),
        'w_ffn2_gate': nrm((DEPTH, D, FFN_DIM), D ** -0.5),
        'w_ffn2_up': nrm((DEPTH, D, FFN_DIM), D ** -0.5),
        'w_ffn2_down': nrm((DEPTH, FFN_DIM, D), FFN_DIM ** -0.5),
    }


def reference(x_prompt, x_sample, cache_k, cache_v, c, c_ctx, w_mod, b_mod,
              g_ffn1, w_ffn1_gate, w_ffn1_up, w_ffn1_down, g_mix, w_in, g_q, g_k, rpb,
              w_pool, pool_scale, w_conv, b_conv, w_br_attn, w_br_pool, w_br_conv, w_out,
              g_ffn2, w_ffn2_gate, w_ffn2_up, w_ffn2_down):
    xp = x_prompt
    xs = x_sample
    new_k = []
    new_v = []
    for l in range(DEPTH):
        lp = {
            'g_ffn1': g_ffn1[l], 'w_ffn1_gate': w_ffn1_gate[l], 'w_ffn1_up': w_ffn1_up[l],
            'w_ffn1_down': w_ffn1_down[l], 'g_mix': g_mix[l], 'w_in': w_in[l],
            'g_q': g_q[l], 'g_k': g_k[l], 'w_pool': w_pool[l], 'pool_scale': pool_scale[l],
            'w_conv': w_conv[l], 'b_conv': b_conv[l], 'w_br_attn': w_br_attn[l],
            'w_br_pool': w_br_pool[l], 'w_br_conv': w_br_conv[l], 'w_out': w_out[l],
            'g_ffn2': g_ffn2[l], 'w_ffn2_gate': w_ffn2_gate[l], 'w_ffn2_up': w_ffn2_up[l],
            'w_ffn2_down': w_ffn2_down[l],
        }
        mod_ctx = adaln(c_ctx[None, :], w_mod[l], b_mod[l])
        xp, kp, vp = trunk_layer(xp, mod_ctx, lp, context_attention)
        new_k.append(kp)
        new_v.append(vp)
        mod_lat = adaln(c, w_mod[l], b_mod[l])
        attend_lat = functools.partial(neighbourhood_attention, k_ctx=cache_k[:, l],
                                       v_ctx=cache_v[:, l], rpb=rpb[l])
        xs, _, _ = trunk_layer(xs, mod_lat, lp, attend_lat)
    return (xp, xs, jnp.stack(new_k, axis=1), jnp.stack(new_v, axis=1))
```

```python
import functools

import numpy as np
import jax
import jax.numpy as jnp
from jax import lax
from jax.experimental import pallas as pl
from jax.experimental.pallas import tpu as pltpu

D_MODEL = 1024
DEPTH = 2
GRID_W = 64
WIN_ROWS = 8
WIN_COLS = 16
N_HEADS = 8
HEAD_DIM = 64
ATTN_DIM = N_HEADS * HEAD_DIM
POOL_GROUPS = 4
POOL_GROUP_DIM = 64
POOL_DIM = POOL_GROUPS * POOL_GROUP_DIM
POOL_WINDOWS = (2, 4, 8, 16)
CONV_DIM = 256
FFN_DIM = 2816
N_MOD = 9
EPS = 1e-6
PROJ_COLS = 3 * ATTN_DIM + POOL_DIM + 3 * CONV_DIM
MERGE_COLS = 3 * D_MODEL
PCZ_COLS = POOL_DIM + 2 * CONV_DIM

LANES = 128
SUBLANES = 8
HALO = SUBLANES
MASK_VALUE = -1e30
VMEM_LIMIT = 56 * 1024 * 1024

FFN_TILE = 512
FFN_CHUNK = FFN_DIM // 2
PROJ_TILE = 512
NA_ROWS = 8

F32 = jnp.float32
BF16 = jnp.bfloat16


def _dot(a, b):
    return jnp.dot(a, b, preferred_element_type=F32)


def _dot_nt(a, b):
    return lax.dot_general(a, b, (((1,), (1,)), ((), ())), preferred_element_type=F32)


def _split_bf16(x):
    hi = x.astype(BF16)
    lo = (x - hi.astype(F32)).astype(BF16)
    return hi, lo


def _norm_mod(x, g, shift, scale):
    ms = jnp.mean(x * x, axis=-1, keepdims=True)
    y = x * lax.rsqrt(ms + EPS) * g
    return y * (1.0 + scale) + shift


def _const_spec(shape):
    nd = len(shape)
    return pl.BlockSpec(shape, lambda *_: (0,) * nd, pipeline_mode=pl.Buffered(1))


def _params(n_axes):
    return pltpu.CompilerParams(dimension_semantics=("arbitrary",) * n_axes,
                                vmem_limit_bytes=VMEM_LIMIT)


def _mod_kernel(c_ref, w_ref, b_ref, o_ref):
    c = c_ref[...]
    a = c * jax.nn.sigmoid(c)
    a_hi, a_lo = _split_bf16(a)
    w_hi, w_lo = _split_bf16(w_ref[0])
    acc = _dot(a_hi, w_hi) + _dot(a_lo, w_hi) + _dot(a_hi, w_lo)
    o_ref[0] = acc + b_ref[0]


def _mod_table(cvec, w_mod, b_mod):
    nb = D_MODEL
    cols = N_MOD * D_MODEL
    return pl.pallas_call(
        _mod_kernel,
        grid=(DEPTH, cols // nb),
        in_specs=[pl.BlockSpec((SUBLANES, D_MODEL), lambda l, n: (0, 0)),
                  pl.BlockSpec((1, D_MODEL, nb), lambda l, n: (l, 0, n)),
                  pl.BlockSpec((1, 1, nb), lambda l, n: (l, 0, n))],
        out_specs=pl.BlockSpec((1, SUBLANES, nb), lambda l, n: (l, 0, n)),
        out_shape=jax.ShapeDtypeStruct((DEPTH, SUBLANES, cols), F32),
        compiler_params=_params(2),
        name="adaln_table",
    )(cvec, w_mod, b_mod.reshape(DEPTH, 1, cols))


def _ffn_kernel(x_ref, m_ref, g_ref, wg_ref, wu_ref, wd_ref, o_ref, *, mi):
    x = x_ref[...]
    shift = m_ref[0, mi:mi + 1, :]
    scale = m_ref[0, mi + 1:mi + 2, :]
    gate = m_ref[0, mi + 2:mi + 3, :]
    h = _norm_mod(x, g_ref[...], shift, scale).astype(BF16)
    acc = None
    for c in range(FFN_DIM // FFN_CHUNK):
        sl = slice(c * FFN_CHUNK, (c + 1) * FFN_CHUNK)
        a = _dot(h, wg_ref[:, sl])
        b = _dot(h, wu_ref[:, sl])
        s = (a * jax.nn.sigmoid(a) * b).astype(BF16)
        y = _dot(s, wd_ref[sl, :])
        acc = y if acc is None else acc + y
    o_ref[...] = x + (0.5 * gate) * acc


def _ffn(x, mod, g, wg, wu, wd, mi, seq_len):
    t = x.shape[0]
    tm = FFN_TILE
    if mod.shape[0] == 1:
        mod_map = lambda i: (0, 0, 0)
    else:
        tiles_per_group = seq_len // tm
        mod_map = lambda i: (i // tiles_per_group, 0, 0)
    return pl.pallas_call(
        functools.partial(_ffn_kernel, mi=mi),
        grid=(t // tm,),
        in_specs=[pl.BlockSpec((tm, D_MODEL), lambda i: (i, 0)),
                  pl.BlockSpec((1, N_MOD, D_MODEL), mod_map),
                  _const_spec((1, D_MODEL)),
                  _const_spec((D_MODEL, FFN_DIM)),
                  _const_spec((D_MODEL, FFN_DIM)),
                  _const_spec((FFN_DIM, D_MODEL))],
        out_specs=pl.BlockSpec((tm, D_MODEL), lambda i: (i, 0)),
        out_shape=jax.ShapeDtypeStruct((t, D_MODEL), F32),
        compiler_params=_params(1),
        name="ffn",
    )(x, mod, g, wg, wu, wd)


def _proj_kernel(x_ref, m_ref, g_ref, w_ref, ones_ref, gq_ref, gk_ref,
                 q_ref, k_ref, v_ref, pcz_ref):
    x = x_ref[...]
    n = _norm_mod(x, g_ref[...], m_ref[0, 3:4, :], m_ref[0, 4:5, :]).astype(BF16)
    proj = _dot(n, w_ref[...])
    ones = ones_ref[...]

    def head_norm(t, gain):
        hi, lo = _split_bf16(t * t)
        ss = _dot(hi, ones) + _dot(lo, ones)
        return t * lax.rsqrt(ss * (1.0 / HEAD_DIM) + EPS) * gain

    a = ATTN_DIM
    q = head_norm(proj[:, 0:a], gq_ref[...])
    q_ref[...] = (q * (HEAD_DIM ** -0.5)).astype(q_ref.dtype)
    k_ref[...] = head_norm(proj[:, a:2 * a], gk_ref[...]).astype(k_ref.dtype)
    v_ref[...] = proj[:, 2 * a:3 * a].astype(v_ref.dtype)
    o = 3 * a
    u_pool = proj[:, o:o + POOL_DIM]
    u_conv = proj[:, o + POOL_DIM:o + POOL_DIM + CONV_DIM]
    gate_b = proj[:, o + POOL_DIM + CONV_DIM:o + POOL_DIM + 2 * CONV_DIM]
    gate_c = proj[:, o + POOL_DIM + 2 * CONV_DIM:o + POOL_DIM + 3 * CONV_DIM]
    pcz_ref[:, 0:POOL_DIM] = u_pool
    pcz_ref[:, POOL_DIM:POOL_DIM + CONV_DIM] = gate_c * u_conv
    pcz_ref[:, POOL_DIM + CONV_DIM:PCZ_COLS] = gate_b


def _proj(x, mod, g, w, ones, gq, gk, seq_len, kv_dtype):
    t = x.shape[0]
    tm = PROJ_TILE
    if mod.shape[0] == 1:
        mod_map = lambda i: (0, 0, 0)
    else:
        tiles_per_group = seq_len // tm
        mod_map = lambda i: (i // tiles_per_group, 0, 0)
    row_spec = lambda c: pl.BlockSpec((tm, c), lambda i: (i, 0))
    return pl.pallas_call(
        _proj_kernel,
        grid=(t // tm,),
        in_specs=[row_spec(D_MODEL),
                  pl.BlockSpec((1, N_MOD, D_MODEL), mod_map),
                  _const_spec((1, D_MODEL)),
                  _const_spec((D_MODEL, PROJ_COLS)),
                  _const_spec((ATTN_DIM, ATTN_DIM)),
                  _const_spec((1, ATTN_DIM)),
                  _const_spec((1, ATTN_DIM))],
        out_specs=[row_spec(ATTN_DIM), row_spec(ATTN_DIM), row_spec(ATTN_DIM), row_spec(PCZ_COLS)],
        out_shape=[jax.ShapeDtypeStruct((t, ATTN_DIM), BF16),
                   jax.ShapeDtypeStruct((t, ATTN_DIM), kv_dtype),
                   jax.ShapeDtypeStruct((t, ATTN_DIM), kv_dtype),
                   jax.ShapeDtypeStruct((t, PCZ_COLS), F32)],
        compiler_params=_params(1),
        name="proj",
    )(x, mod, g, w, ones, gq, gk)


def _first_head_lanes():
    return lax.broadcasted_iota(jnp.int32, (1, LANES), 1) < HEAD_DIM


def _ctx_attn_kernel(q_ref, k_ref, v_ref, o_ref):
    first = _first_head_lanes()
    for t in range(ATTN_DIM // LANES):
        sl = slice(t * LANES, (t + 1) * LANES)
        qt = q_ref[:, sl]
        kt = k_ref[:, sl].astype(BF16)
        vt = v_ref[:, sl].astype(BF16)
        outs = []
        for half in range(2):
            keep = first if half == 0 else jnp.logical_not(first)
            s = _dot_nt(jnp.where(keep, qt, jnp.zeros_like(qt)), kt)
            e = jnp.exp(s - jnp.max(s, axis=-1, keepdims=True))
            den = jnp.sum(e, axis=-1, keepdims=True)
            outs.append(_dot(e.astype(BF16), vt) / den)
        o_ref[:, sl] = jnp.where(first, outs[0], outs[1]).astype(o_ref.dtype)


def _ctx_attention(q, k, v, seq_len):
    t = q.shape[0]
    spec = pl.BlockSpec((seq_len, ATTN_DIM), lambda i: (i, 0))
    return pl.pallas_call(
        _ctx_attn_kernel,
        grid=(t // seq_len,),
        in_specs=[spec, spec, spec],
        out_specs=spec,
        out_shape=jax.ShapeDtypeStruct((t, ATTN_DIM), BF16),
        compiler_params=_params(1),
        name="ctx_attention",
    )(q, k, v)


def _na_kernel(q_ref, k_ref, v_ref, kc_ref, vc_ref, bias_ref, o_ref, *, n_rows):
    first = _first_head_lanes()
    win_keys = WIN_ROWS * GRID_W
    j = pl.program_id(1)

    def row(r, carry):
        i = j * NA_ROWS + r
        rs = jnp.clip(i - WIN_ROWS // 2, 0, n_rows - WIN_ROWS)
        off0 = rs - i + (WIN_ROWS - 1)
        kstart = pl.multiple_of(rs * GRID_W, GRID_W)
        qstart = pl.multiple_of(r * GRID_W, GRID_W)
        for t in range(ATTN_DIM // LANES):
            sl = slice(t * LANES, (t + 1) * LANES)
            qt = q_ref[0, pl.ds(qstart, GRID_W), sl]
            kt = k_ref[0, pl.ds(kstart, win_keys), sl]
            vt = v_ref[0, pl.ds(kstart, win_keys), sl]
            kct = kc_ref[0, :, sl]
            vct = vc_ref[0, :, sl]
            outs = []
            for half in range(2):
                h = 2 * t + half
                keep = first if half == 0 else jnp.logical_not(first)
                qm = jnp.where(keep, qt, jnp.zeros_like(qt))
                bias = jnp.concatenate([bias_ref[h, off0 + 2 * p] for p in range(WIN_ROWS // 2)], axis=1)
                s_loc = _dot_nt(qm, kt) + bias
                s_ctx = _dot_nt(qm, kct)
                m = jnp.maximum(jnp.max(s_loc, axis=-1, keepdims=True),
                                jnp.max(s_ctx, axis=-1, keepdims=True))
                e_loc = jnp.exp(s_loc - m)
                e_ctx = jnp.exp(s_ctx - m)
                den = jnp.sum(e_loc, axis=-1, keepdims=True) + jnp.sum(e_ctx, axis=-1, keepdims=True)
                o = _dot(e_loc.astype(BF16), vt) + _dot(e_ctx.astype(BF16), vct)
                outs.append(o / den)
            o_ref[0, pl.ds(qstart, GRID_W), sl] = jnp.where(first, outs[0], outs[1]).astype(o_ref.dtype)
        return carry

    lax.fori_loop(0, NA_ROWS, row, 0)


def _na_attention(q, k, v, kc, vc, bias):
    b, t, _ = q.shape
    past = kc.shape[1]
    n_rows = t // GRID_W
    q_spec = pl.BlockSpec((1, NA_ROWS * GRID_W, ATTN_DIM), lambda bi, j: (bi, j, 0))
    img_spec = pl.BlockSpec((1, t, ATTN_DIM), lambda bi, j: (bi, 0, 0))
    past_spec = pl.BlockSpec((1, past, ATTN_DIM), lambda bi, j: (bi, 0, 0))
    return pl.pallas_call(
        functools.partial(_na_kernel, n_rows=n_rows),
        grid=(b, n_rows // NA_ROWS),
        in_specs=[q_spec, img_spec, img_spec, past_spec, past_spec, _const_spec(bias.shape)],
        out_specs=q_spec,
        out_shape=jax.ShapeDtypeStruct((b, t, ATTN_DIM), BF16),
        compiler_params=_params(2),
        name="na_attention",
    )(q, k, v, kc, vc, bias)


def _na_bias_table(rpb):
    qc = np.arange(GRID_W)[:, None]
    kc = np.arange(GRID_W)[None, :]
    start = np.clip(qc - WIN_COLS // 2, 0, GRID_W - WIN_COLS)
    inside = (kc >= start) & (kc < start + WIN_COLS)
    idx = np.clip(kc - qc + WIN_COLS - 1, 0, 2 * WIN_COLS - 2)
    dense = jnp.where(inside, rpb[:, :, idx], MASK_VALUE)
    return jnp.concatenate([dense[:, :-1], dense[:, 1:]], axis=-1)


def _merge_kernel(x_ref, m_ref, g_ref, wm_ref, a_ref, pcz_ref, prev_ref, next_ref,
                  wpool_ref, pscale_ref, wconv_ref, bconv_ref, wa_ref, wp_ref, wc_ref, wo_ref,
                  o_ref, buf_ref, *, seq_len):
    tm = x_ref.shape[1]
    j = pl.program_id(1)
    n_tiles = pl.num_programs(1)

    buf_ref[HALO:HALO + tm, :] = pcz_ref[0]
    buf_ref[0:HALO, :] = jnp.where(j > 0, prev_ref[0], 0.0)
    buf_ref[HALO + tm:HALO + tm + HALO, :] = jnp.where(j < n_tiles - 1, next_ref[0], 0.0)

    def shifted(d, cols):
        return buf_ref[HALO + d:HALO + d + tm, cols]

    lane = lax.broadcasted_iota(jnp.int32, (1, LANES), 1)
    upper = lane >= POOL_GROUP_DIM
    pos = j * tm + lax.broadcasted_iota(jnp.int32, (tm, 1), 0)
    diffs = []
    for p in range(POOL_DIM // LANES):
        cols = slice(p * LANES, (p + 1) * LANES)
        w_lo, w_hi = POOL_WINDOWS[2 * p], POOL_WINDOWS[2 * p + 1]
        centre = shifted(0, cols)
        common = centre
        for d in range(-(w_lo // 2), w_lo // 2):
            if d != 0:
                common = common + shifted(d, cols)
        extra = None
        for d in list(range(-(w_hi // 2), -(w_lo // 2))) + list(range(w_lo // 2, w_hi // 2)):
            extra = shifted(d, cols) if extra is None else extra + shifted(d, cols)
        total = common + jnp.where(upper, extra, 0.0)
        half = jnp.where(upper, w_hi // 2, w_lo // 2)
        count = jnp.minimum(pos + half, seq_len) - jnp.maximum(pos - half, 0)
        diffs.append(total / count.astype(F32) - centre)
    diff = jnp.concatenate(diffs, axis=1).astype(BF16)
    pooled = _dot(diff, wpool_ref[...]) * pscale_ref[...]

    zc = slice(POOL_DIM, POOL_DIM + CONV_DIM)
    conv = (wconv_ref[0:1, :] * shifted(-1, zc) + wconv_ref[1:2, :] * shifted(0, zc)
            + wconv_ref[2:3, :] * shifted(1, zc) + bconv_ref[...])
    convd = shifted(0, slice(POOL_DIM + CONV_DIM, PCZ_COLS)) * conv

    x = x_ref[0]
    n = _norm_mod(x, g_ref[...], m_ref[0, 3:4, :], m_ref[0, 4:5, :]).astype(BF16)
    gates = jax.nn.sigmoid(_dot(n, wm_ref[...]))
    d = D_MODEL
    merged = (gates[:, 0:d] * _dot(a_ref[0], wa_ref[...])
              + gates[:, d:2 * d] * _dot(pooled.astype(BF16), wp_ref[...])
              + gates[:, 2 * d:3 * d] * _dot(convd.astype(BF16), wc_ref[...]))
    o_ref[0] = x + m_ref[0, 5:6, :] * _dot(merged.astype(BF16), wo_ref[...])


def _merge(x, mod, g, wm, a, pcz, wpool, pscale, wconv, bconv, wa, wp, wc, wo, tm):
    b, seq_len, _ = x.shape
    n_tiles = seq_len // tm
    halo_per_tile = tm // HALO
    n_halo = seq_len // HALO
    mod_map = (lambda bi, j: (0, 0, 0)) if mod.shape[0] == 1 else (lambda bi, j: (bi, 0, 0))
    tile = lambda c: pl.BlockSpec((1, tm, c), lambda bi, j: (bi, j, 0))
    prev_spec = pl.BlockSpec((1, HALO, PCZ_COLS),
                             lambda bi, j: (bi, jnp.maximum(j * halo_per_tile - 1, 0), 0))
    next_spec = pl.BlockSpec((1, HALO, PCZ_COLS),
                             lambda bi, j: (bi, jnp.minimum((j + 1) * halo_per_tile, n_halo - 1), 0))
    return pl.pallas_call(
        functools.partial(_merge_kernel, seq_len=seq_len),
        grid=(b, n_tiles),
        in_specs=[tile(D_MODEL),
                  pl.BlockSpec((1, N_MOD, D_MODEL), mod_map),
                  _const_spec((1, D_MODEL)),
                  _const_spec((D_MODEL, MERGE_COLS)),
                  tile(ATTN_DIM), tile(PCZ_COLS), prev_spec, next_spec,
                  _const_spec((POOL_DIM, POOL_DIM)),
                  _const_spec((1, POOL_DIM)),
                  _const_spec((3, CONV_DIM)),
                  _const_spec((1, CONV_DIM)),
                  _const_spec((ATTN_DIM, D_MODEL)),
                  _const_spec((POOL_DIM, D_MODEL)),
                  _const_spec((CONV_DIM, D_MODEL)),
                  _const_spec((D_MODEL, D_MODEL))],
        out_specs=tile(D_MODEL),
        out_shape=jax.ShapeDtypeStruct((b, seq_len, D_MODEL), F32),
        scratch_shapes=[pltpu.VMEM((tm + 2 * HALO, PCZ_COLS), F32)],
        compiler_params=_params(2),
        name="merge",
    )(x, mod, g, wm, a, pcz, pcz, pcz, wpool, pscale, wconv, bconv, wa, wp, wc, wo)


def _head_ones():
    blk = np.arange(ATTN_DIM) // HEAD_DIM
    return jnp.asarray(blk[:, None] == blk[None, :], dtype=BF16)


def _layer(x, mod, lw, attend, merge_tile, kv_dtype):
    b, seq_len, _ = x.shape
    t = b * seq_len
    x2 = x.reshape(t, D_MODEL)
    x2 = _ffn(x2, mod, lw['g_ffn1'], lw['w1g'], lw['w1u'], lw['w1d'], 0, seq_len)
    q, k, v, pcz = _proj(x2, mod, lw['g_mix'], lw['w_proj'], lw['ones'], lw['g_q'], lw['g_k'],
                         seq_len, kv_dtype)
    a = attend(q, k, v)
    x3 = _merge(x2.reshape(b, seq_len, D_MODEL), mod, lw['g_mix'], lw['w_merge'],
                a.reshape(b, seq_len, ATTN_DIM), pcz.reshape(b, seq_len, PCZ_COLS),
                lw['w_pool'], lw['pool_scale'], lw['w_conv'], lw['b_conv'],
                lw['w_br_attn'], lw['w_br_pool'], lw['w_br_conv'], lw['w_out'], merge_tile)
    x2 = _ffn(x3.reshape(t, D_MODEL), mod, lw['g_ffn2'], lw['w2g'], lw['w2u'], lw['w2d'], 6, seq_len)
    return x2.reshape(b, seq_len, D_MODEL), k, v


def kernel(x_prompt, x_sample, cache_k, cache_v, c, c_ctx, w_mod, b_mod, g_ffn1, w_ffn1_gate, w_ffn1_up, w_ffn1_down, g_mix, w_in, g_q, g_k, rpb, w_pool, pool_scale, w_conv, b_conv, w_br_attn, w_br_pool, w_br_conv, w_out, g_ffn2, w_ffn2_gate, w_ffn2_up, w_ffn2_down):
    batch, seq, _ = x_prompt.shape
    dec_batch, dec_seq, _ = x_sample.shape
    past = cache_k.shape[3]
    assert dec_batch + 1 <= SUBLANES

    cvec = jnp.zeros((SUBLANES, D_MODEL), F32).at[0].set(c_ctx).at[1:1 + dec_batch].set(c)
    mod = _mod_table(cvec, w_mod, b_mod).reshape(DEPTH, SUBLANES, N_MOD, D_MODEL)

    ones = _head_ones()
    group_eye = jnp.eye(POOL_GROUPS, dtype=F32)
    xp, xs = x_prompt, x_sample
    new_k, new_v = [], []
    for l in range(DEPTH):
        row = lambda p: p[l].reshape(1, -1)
        lw = {
            'g_ffn1': row(g_ffn1), 'g_mix': row(g_mix), 'g_ffn2': row(g_ffn2),
            'w1g': w_ffn1_gate[l].astype(BF16), 'w1u': w_ffn1_up[l].astype(BF16),
            'w1d': w_ffn1_down[l].astype(BF16),
            'w2g': w_ffn2_gate[l].astype(BF16), 'w2u': w_ffn2_up[l].astype(BF16),
            'w2d': w_ffn2_down[l].astype(BF16),
            'w_proj': w_in[l, :, :PROJ_COLS].astype(BF16),
            'w_merge': w_in[l, :, PROJ_COLS:].astype(BF16),
            'ones': ones,
            'g_q': jnp.tile(g_q[l], N_HEADS).reshape(1, ATTN_DIM),
            'g_k': jnp.tile(g_k[l], N_HEADS).reshape(1, ATTN_DIM),
            'w_pool': (group_eye[:, None, :, None] * w_pool[l][:, :, None, :]
                       ).reshape(POOL_DIM, POOL_DIM).astype(BF16),
            'pool_scale': row(pool_scale), 'w_conv': w_conv[l], 'b_conv': row(b_conv),
            'w_br_attn': w_br_attn[l].astype(BF16), 'w_br_pool': w_br_pool[l].astype(BF16),
            'w_br_conv': w_br_conv[l].astype(BF16), 'w_out': w_out[l].astype(BF16),
        }
        attend_ctx = functools.partial(_ctx_attention, seq_len=seq)
        xp, kp, vp = _layer(xp, mod[l, 0:1], lw, attend_ctx, seq, F32)
        to_heads = lambda z: z.reshape(batch, seq, N_HEADS, HEAD_DIM).transpose(0, 2, 1, 3)
        new_k.append(to_heads(kp))
        new_v.append(to_heads(vp))
        from_heads = lambda z: z.transpose(0, 2, 1, 3).reshape(dec_batch, past, ATTN_DIM).astype(BF16)
        kc, vc = from_heads(cache_k[:, l]), from_heads(cache_v[:, l])
        bias = _na_bias_table(rpb[l])

        def attend_lat(q, k, v):
            r3 = lambda z: z.reshape(dec_batch, dec_seq, ATTN_DIM)
            return _na_attention(r3(q), r3(k), r3(v), kc, vc, bias).reshape(dec_batch * dec_seq, ATTN_DIM)

        xs, _, _ = _layer(xs, mod[l, 1:1 + dec_batch], lw, attend_lat, 512, BF16)
    return (xp, xs, jnp.stack(new_k, axis=1), jnp.stack(new_v, axis=1))
```

```python
import functools

import numpy as np
import jax
import jax.numpy as jnp
from jax import lax
from jax.experimental import pallas as pl
from jax.experimental.pallas import tpu as pltpu

D_MODEL = 1024
DEPTH = 2
GRID_W = 64
WIN_ROWS = 8
WIN_COLS = 16
N_HEADS = 8
HEAD_DIM = 64
ATTN_DIM = N_HEADS * HEAD_DIM
POOL_GROUPS = 4
POOL_GROUP_DIM = 64
POOL_DIM = POOL_GROUPS * POOL_GROUP_DIM
POOL_WINDOWS = (2, 4, 8, 16)
CONV_DIM = 256
FFN_DIM = 2816
N_MOD = 9
EPS = 1e-6
PROJ_COLS = 3 * ATTN_DIM + POOL_DIM + 3 * CONV_DIM
MERGE_COLS = 3 * D_MODEL
PCZ_COLS = POOL_DIM + 2 * CONV_DIM

LANES = 128
SUBLANES = 8
HALO = SUBLANES
MASK_VALUE = -1e30
VMEM_LIMIT = 56 * 1024 * 1024

FFN_TILE = 512
MXU_DIM = 256
FFN_SPLITS = (0, (FFN_DIM // MXU_DIM // 2) * MXU_DIM, FFN_DIM)
PROJ_TILE = 512
NA_ROWS = 8
NA_CTX_CHUNK = 256

F32 = jnp.float32
BF16 = jnp.bfloat16


def _dot(a, b):
    return jnp.dot(a, b, preferred_element_type=F32)


def _dot_nt(a, b):
    return lax.dot_general(a, b, (((1,), (1,)), ((), ())), preferred_element_type=F32)


def _split_bf16(x):
    hi = x.astype(BF16)
    lo = (x - hi.astype(F32)).astype(BF16)
    return hi, lo


def _norm_mod(x, g, shift, scale):
    ms = jnp.mean(x * x, axis=-1, keepdims=True)
    y = x * lax.rsqrt(ms + EPS) * g
    return y * (1.0 + scale) + shift


def _const_spec(shape):
    nd = len(shape)
    return pl.BlockSpec(shape, lambda *_: (0,) * nd, pipeline_mode=pl.Buffered(1))


def _params(n_axes):
    return pltpu.CompilerParams(dimension_semantics=("arbitrary",) * n_axes,
                                vmem_limit_bytes=VMEM_LIMIT)


def _mod_kernel(c_ref, w_ref, b_ref, o_ref):
    c = c_ref[...]
    a = c * jax.nn.sigmoid(c)
    a_hi, a_lo = _split_bf16(a)
    w_hi, w_lo = _split_bf16(w_ref[0])
    acc = _dot(a_hi, w_hi) + _dot(a_lo, w_hi) + _dot(a_hi, w_lo)
    o_ref[0] = acc + b_ref[0]


def _mod_table(cvec, w_mod, b_mod):
    nb = D_MODEL
    cols = N_MOD * D_MODEL
    return pl.pallas_call(
        _mod_kernel,
        grid=(DEPTH, cols // nb),
        in_specs=[pl.BlockSpec((SUBLANES, D_MODEL), lambda l, n: (0, 0)),
                  pl.BlockSpec((1, D_MODEL, nb), lambda l, n: (l, 0, n)),
                  pl.BlockSpec((1, 1, nb), lambda l, n: (l, 0, n))],
        out_specs=pl.BlockSpec((1, SUBLANES, nb), lambda l, n: (l, 0, n)),
        out_shape=jax.ShapeDtypeStruct((DEPTH, SUBLANES, cols), F32),
        compiler_params=_params(2),
        name="adaln_table",
    )(cvec, w_mod, b_mod.reshape(DEPTH, 1, cols))


def _ffn_kernel(x_ref, m_ref, g_ref, wg_ref, wu_ref, wd_ref, o_ref, *, mi):
    x = x_ref[...]
    shift = m_ref[0, mi:mi + 1, :]
    scale = m_ref[0, mi + 1:mi + 2, :]
    gate = m_ref[0, mi + 2:mi + 3, :]
    h = _norm_mod(x, g_ref[...], shift, scale).astype(BF16)
    acc = None
    for lo, hi in zip(FFN_SPLITS[:-1], FFN_SPLITS[1:]):
        sl = slice(lo, hi)
        a = _dot(h, wg_ref[:, sl])
        b = _dot(h, wu_ref[:, sl])
        s = (a * jax.nn.sigmoid(a) * b).astype(BF16)
        y = _dot(s, wd_ref[sl, :])
        acc = y if acc is None else acc + y
    o_ref[...] = x + (0.5 * gate) * acc


def _ffn(x, mod, g, wg, wu, wd, mi, seq_len):
    t = x.shape[0]
    tm = FFN_TILE
    if mod.shape[0] == 1:
        mod_map = lambda i: (0, 0, 0)
    else:
        tiles_per_group = seq_len // tm
        mod_map = lambda i: (i // tiles_per_group, 0, 0)
    return pl.pallas_call(
        functools.partial(_ffn_kernel, mi=mi),
        grid=(t // tm,),
        in_specs=[pl.BlockSpec((tm, D_MODEL), lambda i: (i, 0)),
                  pl.BlockSpec((1, N_MOD, D_MODEL), mod_map),
                  _const_spec((1, D_MODEL)),
                  _const_spec((D_MODEL, FFN_DIM)),
                  _const_spec((D_MODEL, FFN_DIM)),
                  _const_spec((FFN_DIM, D_MODEL))],
        out_specs=pl.BlockSpec((tm, D_MODEL), lambda i: (i, 0)),
        out_shape=jax.ShapeDtypeStruct((t, D_MODEL), F32),
        compiler_params=_params(1),
        name="ffn",
    )(x, mod, g, wg, wu, wd)


def _proj_kernel(x_ref, m_ref, g_ref, w_ref, ones_ref, gq_ref, gk_ref,
                 q_ref, k_ref, v_ref, pcz_ref):
    x = x_ref[...]
    n = _norm_mod(x, g_ref[...], m_ref[0, 3:4, :], m_ref[0, 4:5, :]).astype(BF16)
    proj = _dot(n, w_ref[...])
    ones = ones_ref[...]

    def head_norm(t, gain):
        hi, lo = _split_bf16(t * t)
        ss = _dot(hi, ones) + _dot(lo, ones)
        return t * lax.rsqrt(ss * (1.0 / HEAD_DIM) + EPS) * gain

    a = ATTN_DIM
    q = head_norm(proj[:, 0:a], gq_ref[...])
    q_ref[...] = (q * (HEAD_DIM ** -0.5)).astype(q_ref.dtype)
    k_ref[...] = head_norm(proj[:, a:2 * a], gk_ref[...]).astype(k_ref.dtype)
    v_ref[...] = proj[:, 2 * a:3 * a].astype(v_ref.dtype)
    o = 3 * a
    u_pool = proj[:, o:o + POOL_DIM]
    u_conv = proj[:, o + POOL_DIM:o + POOL_DIM + CONV_DIM]
    gate_b = proj[:, o + POOL_DIM + CONV_DIM:o + POOL_DIM + 2 * CONV_DIM]
    gate_c = proj[:, o + POOL_DIM + 2 * CONV_DIM:o + POOL_DIM + 3 * CONV_DIM]
    pcz_ref[:, 0:POOL_DIM] = u_pool
    pcz_ref[:, POOL_DIM:POOL_DIM + CONV_DIM] = gate_c * u_conv
    pcz_ref[:, POOL_DIM + CONV_DIM:PCZ_COLS] = gate_b


def _proj(x, mod, g, w, ones, gq, gk, seq_len, kv_dtype):
    t = x.shape[0]
    tm = PROJ_TILE
    if mod.shape[0] == 1:
        mod_map = lambda i: (0, 0, 0)
    else:
        tiles_per_group = seq_len // tm
        mod_map = lambda i: (i // tiles_per_group, 0, 0)
    row_spec = lambda c: pl.BlockSpec((tm, c), lambda i: (i, 0))
    return pl.pallas_call(
        _proj_kernel,
        grid=(t // tm,),
        in_specs=[row_spec(D_MODEL),
                  pl.BlockSpec((1, N_MOD, D_MODEL), mod_map),
                  _const_spec((1, D_MODEL)),
                  _const_spec((D_MODEL, PROJ_COLS)),
                  _const_spec((ATTN_DIM, ATTN_DIM)),
                  _const_spec((1, ATTN_DIM)),
                  _const_spec((1, ATTN_DIM))],
        out_specs=[row_spec(ATTN_DIM), row_spec(ATTN_DIM), row_spec(ATTN_DIM), row_spec(PCZ_COLS)],
        out_shape=[jax.ShapeDtypeStruct((t, ATTN_DIM), BF16),
                   jax.ShapeDtypeStruct((t, ATTN_DIM), kv_dtype),
                   jax.ShapeDtypeStruct((t, ATTN_DIM), kv_dtype),
                   jax.ShapeDtypeStruct((t, PCZ_COLS), F32)],
        compiler_params=_params(1),
        name="proj",
    )(x, mod, g, w, ones, gq, gk)


def _first_head_lanes():
    return lax.broadcasted_iota(jnp.int32, (1, LANES), 1) < HEAD_DIM


def _ctx_attn_kernel(q_ref, k_ref, v_ref, o_ref):
    first = _first_head_lanes()
    for t in range(ATTN_DIM // LANES):
        sl = slice(t * LANES, (t + 1) * LANES)
        qt = q_ref[:, sl]
        kt = k_ref[:, sl].astype(BF16)
        vt = v_ref[:, sl].astype(BF16)
        outs = []
        for half in range(2):
            keep = first if half == 0 else jnp.logical_not(first)
            s = _dot_nt(jnp.where(keep, qt, jnp.zeros_like(qt)), kt)
            e = jnp.exp(s - jnp.max(s, axis=-1, keepdims=True))
            den = jnp.sum(e, axis=-1, keepdims=True)
            outs.append(_dot(e.astype(BF16), vt) / den)
        o_ref[:, sl] = jnp.where(first, outs[0], outs[1]).astype(o_ref.dtype)


def _ctx_attention(q, k, v, seq_len):
    t = q.shape[0]
    spec = pl.BlockSpec((seq_len, ATTN_DIM), lambda i: (i, 0))
    return pl.pallas_call(
        _ctx_attn_kernel,
        grid=(t // seq_len,),
        in_specs=[spec, spec, spec],
        out_specs=spec,
        out_shape=jax.ShapeDtypeStruct((t, ATTN_DIM), BF16),
        compiler_params=_params(1),
        name="ctx_attention",
    )(q, k, v)


def _row_max_lanes(s):
    m = s[:, :LANES]
    for p in range(1, s.shape[1] // LANES):
        m = jnp.maximum(m, s[:, p * LANES:(p + 1) * LANES])
    return jnp.broadcast_to(jnp.max(m, axis=-1, keepdims=True), m.shape)


def _exp_shifted(s, m):
    return jnp.concatenate([jnp.exp(s[:, p * LANES:(p + 1) * LANES] - m)
                            for p in range(s.shape[1] // LANES)], axis=1)


def _na_kernel(q_ref, k_ref, v_ref, kc_ref, vc_ref, bias_ref, o_ref,
               qs_ref, oc_ref, mc_ref, lc_ref, *, n_rows):
    first = _first_head_lanes()
    win_keys = WIN_ROWS * GRID_W
    pair_rows = 2 * GRID_W
    n_pairs = ATTN_DIM // LANES
    j = pl.program_id(1)

    past = kc_ref.shape[1]
    ones_past = jnp.ones((past, LANES), BF16)
    for t in range(n_pairs):
        sl = slice(t * LANES, (t + 1) * LANES)
        for r in range(NA_ROWS):
            qt = q_ref[0, r * GRID_W:(r + 1) * GRID_W, sl]
            zero = jnp.zeros_like(qt)
            qs_ref[t, r * pair_rows:r * pair_rows + GRID_W, :] = jnp.where(first, qt, zero)
            qs_ref[t, r * pair_rows + GRID_W:(r + 1) * pair_rows, :] = jnp.where(first, zero, qt)
        kct = kc_ref[0, :, sl]
        vc_aug = jnp.concatenate([vc_ref[0, :, sl], ones_past], axis=1)
        for c in range(qs_ref.shape[1] // NA_CTX_CHUNK):
            rows = slice(c * NA_CTX_CHUNK, (c + 1) * NA_CTX_CHUNK)
            s = _dot_nt(qs_ref[t, rows, :], kct)
            m = _row_max_lanes(s)
            pv = _dot(_exp_shifted(s, m).astype(BF16), vc_aug)
            mc_ref[t, rows, :] = m
            oc_ref[t, rows, :] = pv[:, :LANES]
            lc_ref[t, rows, :] = pv[:, LANES:]

    ones_win = jnp.ones((win_keys, LANES), BF16)

    def row(r, carry):
        i = j * NA_ROWS + r
        rs = jnp.clip(i - WIN_ROWS // 2, 0, n_rows - WIN_ROWS)
        off0 = rs - i + (WIN_ROWS - 1)
        kstart = pl.multiple_of(rs * GRID_W, GRID_W)
        qstart = pl.multiple_of(r * GRID_W, GRID_W)
        rows = pl.ds(pl.multiple_of(r * pair_rows, pair_rows), pair_rows)
        for t in range(n_pairs):
            sl = slice(t * LANES, (t + 1) * LANES)
            kt = k_ref[0, pl.ds(kstart, win_keys), sl]
            v_aug = jnp.concatenate([v_ref[0, pl.ds(kstart, win_keys), sl], ones_win], axis=1)
            bias = jnp.concatenate([bias_ref[t, off0 + 2 * p] for p in range(WIN_ROWS // 2)], axis=1)
            s = _dot_nt(qs_ref[t, rows, :], kt) + bias
            m_ctx = mc_ref[t, rows, :]
            m = jnp.maximum(_row_max_lanes(s), m_ctx)
            w_ctx = jnp.exp(m_ctx - m)
            pv = _dot(_exp_shifted(s, m).astype(BF16), v_aug)
            o = ((pv[:, :LANES] + oc_ref[t, rows, :] * w_ctx)
                 / (pv[:, LANES:] + lc_ref[t, rows, :] * w_ctx))
            o_ref[0, pl.ds(qstart, GRID_W), sl] = jnp.where(first, o[:GRID_W], o[GRID_W:]).astype(o_ref.dtype)
        return carry

    lax.fori_loop(0, NA_ROWS, row, 0, unroll=2)


def _na_attention(q, k, v, kc, vc, bias):
    b, t, _ = q.shape
    past = kc.shape[1]
    n_rows = t // GRID_W
    n_pairs = ATTN_DIM // LANES
    stacked = 2 * NA_ROWS * GRID_W
    q_spec = pl.BlockSpec((1, NA_ROWS * GRID_W, ATTN_DIM), lambda bi, j: (bi, j, 0))
    img_spec = pl.BlockSpec((1, t, ATTN_DIM), lambda bi, j: (bi, 0, 0))
    past_spec = pl.BlockSpec((1, past, ATTN_DIM), lambda bi, j: (bi, 0, 0))
    return pl.pallas_call(
        functools.partial(_na_kernel, n_rows=n_rows),
        grid=(b, n_rows // NA_ROWS),
        in_specs=[q_spec, img_spec, img_spec, past_spec, past_spec, _const_spec(bias.shape)],
        out_specs=q_spec,
        out_shape=jax.ShapeDtypeStruct((b, t, ATTN_DIM), BF16),
        scratch_shapes=[pltpu.VMEM((n_pairs, stacked, LANES), BF16),
                        pltpu.VMEM((n_pairs, stacked, LANES), F32),
                        pltpu.VMEM((n_pairs, stacked, LANES), F32),
                        pltpu.VMEM((n_pairs, stacked, LANES), F32)],
        compiler_params=_params(2),
        name="na_attention",
    )(q, k, v, kc, vc, bias)


def _bias_kernel(r_ref, sel_ref, mask_ref, o_ref):
    r = r_ref[...]
    hi = r.astype(BF16)
    r1 = r - hi.astype(F32)
    mid = r1.astype(BF16)
    lo = (r1 - mid.astype(F32)).astype(BF16)
    sel = sel_ref[...]
    o_ref[...] = _dot(hi, sel) + _dot(mid, sel) + _dot(lo, sel) + mask_ref[...]


def _na_bias_tables(rpb):
    n_rel = 2 * WIN_COLS - 1
    qc = np.arange(GRID_W)[:, None]
    kc = np.arange(GRID_W)[None, :]
    start = np.clip(qc - WIN_COLS // 2, 0, GRID_W - WIN_COLS)
    inside = ((kc >= start) & (kc < start + WIN_COLS)).reshape(-1)
    rel = (kc - qc + WIN_COLS - 1).reshape(-1)
    sel = (np.arange(LANES)[:, None] == rel[None, :]) & inside[None, :]
    mask = np.where(inside, 0.0, MASK_VALUE)[None, :]
    n_off = 2 * WIN_ROWS - 1
    rows = DEPTH * N_HEADS * n_off
    table = jnp.pad(rpb.reshape(rows, n_rel), ((0, 0), (0, LANES - n_rel)))
    dense = pl.pallas_call(
        _bias_kernel,
        out_shape=jax.ShapeDtypeStruct((rows, GRID_W * GRID_W), F32),
        compiler_params=pltpu.CompilerParams(vmem_limit_bytes=VMEM_LIMIT),
        name="na_bias",
    )(table, jnp.asarray(sel, dtype=BF16), jnp.asarray(mask, dtype=F32))
    dense = dense.reshape(DEPTH, N_HEADS // 2, 2, n_off, GRID_W, GRID_W)
    two_rows = jnp.concatenate([dense[:, :, :, :-1], dense[:, :, :, 1:]], axis=-1)
    return two_rows.transpose(0, 1, 3, 2, 4, 5).reshape(
        DEPTH, N_HEADS // 2, n_off - 1, 2 * GRID_W, 2 * GRID_W)


def _merge_kernel(x_ref, m_ref, g_ref, wm_ref, a_ref, pcz_ref, prev_ref, next_ref,
                  wpool_ref, pscale_ref, wconv_ref, bconv_ref, wa_ref, wp_ref, wc_ref, wo_ref,
                  o_ref, buf_ref, *, seq_len):
    tm = x_ref.shape[1]
    j = pl.program_id(1)
    n_tiles = pl.num_programs(1)

    buf_ref[HALO:HALO + tm, :] = pcz_ref[0]
    buf_ref[0:HALO, :] = jnp.where(j > 0, prev_ref[0], 0.0)
    buf_ref[HALO + tm:HALO + tm + HALO, :] = jnp.where(j < n_tiles - 1, next_ref[0], 0.0)

    def shifted(d, cols):
        return buf_ref[HALO + d:HALO + d + tm, cols]

    lane = lax.broadcasted_iota(jnp.int32, (1, LANES), 1)
    upper = lane >= POOL_GROUP_DIM
    pos = j * tm + lax.broadcasted_iota(jnp.int32, (tm, 1), 0)
    diffs = []
    for p in range(POOL_DIM // LANES):
        cols = slice(p * LANES, (p + 1) * LANES)
        w_lo, w_hi = POOL_WINDOWS[2 * p], POOL_WINDOWS[2 * p + 1]
        centre = shifted(0, cols)
        common = centre
        for d in range(-(w_lo // 2), w_lo // 2):
            if d != 0:
                common = common + shifted(d, cols)
        extra = None
        for d in list(range(-(w_hi // 2), -(w_lo // 2))) + list(range(w_lo // 2, w_hi // 2)):
            extra = shifted(d, cols) if extra is None else extra + shifted(d, cols)
        total = common + jnp.where(upper, extra, 0.0)
        half = jnp.where(upper, w_hi // 2, w_lo // 2)
        count = jnp.minimum(pos + half, seq_len) - jnp.maximum(pos - half, 0)
        diffs.append(total / count.astype(F32) - centre)
    diff = jnp.concatenate(diffs, axis=1).astype(BF16)
    pooled = _dot(diff, wpool_ref[...]) * pscale_ref[...]

    zc = slice(POOL_DIM, POOL_DIM + CONV_DIM)
    conv = (wconv_ref[0:1, :] * shifted(-1, zc) + wconv_ref[1:2, :] * shifted(0, zc)
            + wconv_ref[2:3, :] * shifted(1, zc) + bconv_ref[...])
    convd = shifted(0, slice(POOL_DIM + CONV_DIM, PCZ_COLS)) * conv

    x = x_ref[0]
    n = _norm_mod(x, g_ref[...], m_ref[0, 3:4, :], m_ref[0, 4:5, :]).astype(BF16)
    gates = jax.nn.sigmoid(_dot(n, wm_ref[...]))
    d = D_MODEL
    merged = (gates[:, 0:d] * _dot(a_ref[0], wa_ref[...])
              + gates[:, d:2 * d] * _dot(pooled.astype(BF16), wp_ref[...])
              + gates[:, 2 * d:3 * d] * _dot(convd.astype(BF16), wc_ref[...]))
    o_ref[0] = x + m_ref[0, 5:6, :] * _dot(merged.astype(BF16), wo_ref[...])


def _merge(x, mod, g, wm, a, pcz, wpool, pscale, wconv, bconv, wa, wp, wc, wo, tm):
    b, seq_len, _ = x.shape
    n_tiles = seq_len // tm
    halo_per_tile = tm // HALO
    n_halo = seq_len // HALO
    mod_map = (lambda bi, j: (0, 0, 0)) if mod.shape[0] == 1 else (lambda bi, j: (bi, 0, 0))
    tile = lambda c: pl.BlockSpec((1, tm, c), lambda bi, j: (bi, j, 0))
    prev_spec = pl.BlockSpec((1, HALO, PCZ_COLS),
                             lambda bi, j: (bi, jnp.maximum(j * halo_per_tile - 1, 0), 0))
    next_spec = pl.BlockSpec((1, HALO, PCZ_COLS),
                             lambda bi, j: (bi, jnp.minimum((j + 1) * halo_per_tile, n_halo - 1), 0))
    return pl.pallas_call(
        functools.partial(_merge_kernel, seq_len=seq_len),
        grid=(b, n_tiles),
        in_specs=[tile(D_MODEL),
                  pl.BlockSpec((1, N_MOD, D_MODEL), mod_map),
                  _const_spec((1, D_MODEL)),
                  _const_spec((D_MODEL, MERGE_COLS)),
                  tile(ATTN_DIM), tile(PCZ_COLS), prev_spec, next_spec,
                  _const_spec((POOL_DIM, POOL_DIM)),
                  _const_spec((1, POOL_DIM)),
                  _const_spec((3, CONV_DIM)),
                  _const_spec((1, CONV_DIM)),
                  _const_spec((ATTN_DIM, D_MODEL)),
                  _const_spec((POOL_DIM, D_MODEL)),
                  _const_spec((CONV_DIM, D_MODEL)),
                  _const_spec((D_MODEL, D_MODEL))],
        out_specs=tile(D_MODEL),
        out_shape=jax.ShapeDtypeStruct((b, seq_len, D_MODEL), F32),
        scratch_shapes=[pltpu.VMEM((tm + 2 * HALO, PCZ_COLS), F32)],
        compiler_params=_params(2),
        name="merge",
    )(x, mod, g, wm, a, pcz, pcz, pcz, wpool, pscale, wconv, bconv, wa, wp, wc, wo)


def _head_ones():
    blk = np.arange(ATTN_DIM) // HEAD_DIM
    return jnp.asarray(blk[:, None] == blk[None, :], dtype=BF16)


def _layer(x, mod, lw, attend, merge_tile, kv_dtype):
    b, seq_len, _ = x.shape
    t = b * seq_len
    x2 = x.reshape(t, D_MODEL)
    x2 = _ffn(x2, mod, lw['g_ffn1'], lw['w1g'], lw['w1u'], lw['w1d'], 0, seq_len)
    q, k, v, pcz = _proj(x2, mod, lw['g_mix'], lw['w_proj'], lw['ones'], lw['g_q'], lw['g_k'],
                         seq_len, kv_dtype)
    a = attend(q, k, v)
    x3 = _merge(x2.reshape(b, seq_len, D_MODEL), mod, lw['g_mix'], lw['w_merge'],
                a.reshape(b, seq_len, ATTN_DIM), pcz.reshape(b, seq_len, PCZ_COLS),
                lw['w_pool'], lw['pool_scale'], lw['w_conv'], lw['b_conv'],
                lw['w_br_attn'], lw['w_br_pool'], lw['w_br_conv'], lw['w_out'], merge_tile)
    x2 = _ffn(x3.reshape(t, D_MODEL), mod, lw['g_ffn2'], lw['w2g'], lw['w2u'], lw['w2d'], 6, seq_len)
    return x2.reshape(b, seq_len, D_MODEL), k, v


def kernel(x_prompt, x_sample, cache_k, cache_v, c, c_ctx, w_mod, b_mod, g_ffn1, w_ffn1_gate, w_ffn1_up, w_ffn1_down, g_mix, w_in, g_q, g_k, rpb, w_pool, pool_scale, w_conv, b_conv, w_br_attn, w_br_pool, w_br_conv, w_out, g_ffn2, w_ffn2_gate, w_ffn2_up, w_ffn2_down):
    batch, seq, _ = x_prompt.shape
    dec_batch, dec_seq, _ = x_sample.shape
    past = cache_k.shape[3]
    assert dec_batch + 1 <= SUBLANES

    cvec = jnp.zeros((SUBLANES, D_MODEL), F32).at[0].set(c_ctx).at[1:1 + dec_batch].set(c)
    mod = _mod_table(cvec, w_mod, b_mod).reshape(DEPTH, SUBLANES, N_MOD, D_MODEL)

    ones = _head_ones()
    na_bias = _na_bias_tables(rpb)
    group_eye = jnp.eye(POOL_GROUPS, dtype=F32)
    xp, xs = x_prompt, x_sample
    new_k, new_v = [], []
    for l in range(DEPTH):
        row = lambda p: p[l].reshape(1, -1)
        lw = {
            'g_ffn1': row(g_ffn1), 'g_mix': row(g_mix), 'g_ffn2': row(g_ffn2),
            'w1g': w_ffn1_gate[l].astype(BF16), 'w1u': w_ffn1_up[l].astype(BF16),
            'w1d': w_ffn1_down[l].astype(BF16),
            'w2g': w_ffn2_gate[l].astype(BF16), 'w2u': w_ffn2_up[l].astype(BF16),
            'w2d': w_ffn2_down[l].astype(BF16),
            'w_proj': w_in[l, :, :PROJ_COLS].astype(BF16),
            'w_merge': w_in[l, :, PROJ_COLS:].astype(BF16),
            'ones': ones,
            'g_q': jnp.tile(g_q[l], N_HEADS).reshape(1, ATTN_DIM),
            'g_k': jnp.tile(g_k[l], N_HEADS).reshape(1, ATTN_DIM),
            'w_pool': (group_eye[:, None, :, None] * w_pool[l][:, :, None, :]
                       ).reshape(POOL_DIM, POOL_DIM).astype(BF16),
            'pool_scale': row(pool_scale), 'w_conv': w_conv[l], 'b_conv': row(b_conv),
            'w_br_attn': w_br_attn[l].astype(BF16), 'w_br_pool': w_br_pool[l].astype(BF16),
            'w_br_conv': w_br_conv[l].astype(BF16), 'w_out': w_out[l].astype(BF16),
        }
        attend_ctx = functools.partial(_ctx_attention, seq_len=seq)
        xp, kp, vp = _layer(xp, mod[l, 0:1], lw, attend_ctx, seq, F32)
        to_heads = lambda z: z.reshape(batch, seq, N_HEADS, HEAD_DIM).transpose(0, 2, 1, 3)
        new_k.append(to_heads(kp))
        new_v.append(to_heads(vp))
        from_heads = lambda z: z.transpose(0, 2, 1, 3).reshape(dec_batch, past, ATTN_DIM).astype(BF16)
        kc, vc = from_heads(cache_k[:, l]), from_heads(cache_v[:, l])
        bias = na_bias[l]

        def attend_lat(q, k, v):
            r3 = lambda z: z.reshape(dec_batch, dec_seq, ATTN_DIM)
            return _na_attention(r3(q), r3(k), r3(v), kc, vc, bias).reshape(dec_batch * dec_seq, ATTN_DIM)

        xs, _, _ = _layer(xs, mod[l, 1:1 + dec_batch], lw, attend_lat, 512, BF16)
    return (xp, xs, jnp.stack(new_k, axis=1), jnp.stack(new_v, axis=1))
```

```python
import functools

import numpy as np
import jax
import jax.numpy as jnp
from jax import lax
from jax.experimental import pallas as pl
from jax.experimental.pallas import tpu as pltpu

D_MODEL = 1024
DEPTH = 2
GRID_W = 64
WIN_ROWS = 8
WIN_COLS = 16
N_HEADS = 8
HEAD_DIM = 64
ATTN_DIM = N_HEADS * HEAD_DIM
POOL_GROUPS = 4
POOL_GROUP_DIM = 64
POOL_DIM = POOL_GROUPS * POOL_GROUP_DIM
POOL_WINDOWS = (2, 4, 8, 16)
CONV_DIM = 256
FFN_DIM = 2816
N_MOD = 9
EPS = 1e-6
PROJ_COLS = 3 * ATTN_DIM + POOL_DIM + 3 * CONV_DIM
MERGE_COLS = 3 * D_MODEL
PCZ_COLS = POOL_DIM + 2 * CONV_DIM

LANES = 128
SUBLANES = 8
HALO = SUBLANES
MASK_VALUE = -1e30
VMEM_LIMIT = 56 * 1024 * 1024

FFN_TILE = 512
MXU_DIM = 256
FFN_SPLITS = (0, (FFN_DIM // MXU_DIM // 2) * MXU_DIM, FFN_DIM)
PROJ_TILE = 512
NA_ROWS = 8
NA_CTX_CHUNK = 256

F32 = jnp.float32
BF16 = jnp.bfloat16


def _dot(a, b):
    return jnp.dot(a, b, preferred_element_type=F32)


def _dot_nt(a, b):
    return lax.dot_general(a, b, (((1,), (1,)), ((), ())), preferred_element_type=F32)


def _split_bf16(x):
    hi = x.astype(BF16)
    lo = (x - hi.astype(F32)).astype(BF16)
    return hi, lo


def _norm_mod(x, g, shift, scale):
    ms = jnp.mean(x * x, axis=-1, keepdims=True)
    y = x * lax.rsqrt(ms + EPS) * g
    return y * (1.0 + scale) + shift


def _const_spec(shape):
    nd = len(shape)
    return pl.BlockSpec(shape, lambda *_: (0,) * nd, pipeline_mode=pl.Buffered(1))


def _layer_spec(shape, layer):
    nd = len(shape)
    return pl.BlockSpec((None,) + tuple(shape), lambda *_: (layer,) + (0,) * nd,
                        pipeline_mode=pl.Buffered(1))


def _params(n_axes):
    return pltpu.CompilerParams(dimension_semantics=("arbitrary",) * n_axes,
                                vmem_limit_bytes=VMEM_LIMIT)


def _mod_kernel(c_ref, w_ref, b_ref, o_ref):
    c = c_ref[...]
    a = c * jax.nn.sigmoid(c)
    a_hi, a_lo = _split_bf16(a)
    w_hi, w_lo = _split_bf16(w_ref[0])
    acc = _dot(a_hi, w_hi) + _dot(a_lo, w_hi) + _dot(a_hi, w_lo)
    o_ref[0] = acc + b_ref[0]


def _mod_table(cvec, w_mod, b_mod):
    nb = D_MODEL
    cols = N_MOD * D_MODEL
    return pl.pallas_call(
        _mod_kernel,
        grid=(DEPTH, cols // nb),
        in_specs=[pl.BlockSpec((SUBLANES, D_MODEL), lambda l, n: (0, 0)),
                  pl.BlockSpec((1, D_MODEL, nb), lambda l, n: (l, 0, n)),
                  pl.BlockSpec((1, 1, nb), lambda l, n: (l, 0, n))],
        out_specs=pl.BlockSpec((1, SUBLANES, nb), lambda l, n: (l, 0, n)),
        out_shape=jax.ShapeDtypeStruct((DEPTH, SUBLANES, cols), F32),
        compiler_params=_params(2),
        name="adaln_table",
    )(cvec, w_mod, b_mod.reshape(DEPTH, 1, cols))


def _ffn_kernel(x_ref, m_ref, g_ref, wg_ref, wu_ref, wd_ref, o_ref, *, mi):
    x = x_ref[...]
    shift = m_ref[0, mi:mi + 1, :]
    scale = m_ref[0, mi + 1:mi + 2, :]
    gate = m_ref[0, mi + 2:mi + 3, :]
    h = _norm_mod(x, g_ref[...], shift, scale).astype(BF16)
    acc = None
    for lo, hi in zip(FFN_SPLITS[:-1], FFN_SPLITS[1:]):
        sl = slice(lo, hi)
        a = _dot(h, wg_ref[:, sl])
        b = _dot(h, wu_ref[:, sl])
        s = (a * jax.nn.sigmoid(a) * b).astype(BF16)
        y = _dot(s, wd_ref[sl, :])
        acc = y if acc is None else acc + y
    o_ref[...] = x + (0.5 * gate) * acc


def _ffn(x, mod, g, wg, wu, wd, layer, mi, seq_len):
    t = x.shape[0]
    tm = FFN_TILE
    if mod.shape[0] == 1:
        mod_map = lambda i: (0, 0, 0)
    else:
        tiles_per_group = seq_len // tm
        mod_map = lambda i: (i // tiles_per_group, 0, 0)
    return pl.pallas_call(
        functools.partial(_ffn_kernel, mi=mi),
        grid=(t // tm,),
        in_specs=[pl.BlockSpec((tm, D_MODEL), lambda i: (i, 0)),
                  pl.BlockSpec((1, N_MOD, D_MODEL), mod_map),
                  _const_spec((1, D_MODEL)),
                  _layer_spec((D_MODEL, FFN_DIM), layer),
                  _layer_spec((D_MODEL, FFN_DIM), layer),
                  _layer_spec((FFN_DIM, D_MODEL), layer)],
        out_specs=pl.BlockSpec((tm, D_MODEL), lambda i: (i, 0)),
        out_shape=jax.ShapeDtypeStruct((t, D_MODEL), F32),
        compiler_params=_params(1),
        name="ffn",
    )(x, mod, g, wg, wu, wd)


def _proj_kernel(x_ref, m_ref, g_ref, w_ref, ones_ref, gq_ref, gk_ref,
                 q_ref, k_ref, v_ref, pcz_ref):
    x = x_ref[...]
    n = _norm_mod(x, g_ref[...], m_ref[0, 3:4, :], m_ref[0, 4:5, :]).astype(BF16)
    proj = _dot(n, w_ref[:, :PROJ_COLS])
    ones = ones_ref[...]

    def head_norm(t, gain):
        ss = _dot((t * t).astype(BF16), ones)
        return t * lax.rsqrt(ss * (1.0 / HEAD_DIM) + EPS) * gain

    a = ATTN_DIM
    q = head_norm(proj[:, 0:a], gq_ref[...])
    q_ref[...] = (q * (HEAD_DIM ** -0.5)).astype(q_ref.dtype)
    k_ref[...] = head_norm(proj[:, a:2 * a], gk_ref[...]).astype(k_ref.dtype)
    v_ref[...] = proj[:, 2 * a:3 * a].astype(v_ref.dtype)
    o = 3 * a
    u_pool = proj[:, o:o + POOL_DIM]
    u_conv = proj[:, o + POOL_DIM:o + POOL_DIM + CONV_DIM]
    gate_b = proj[:, o + POOL_DIM + CONV_DIM:o + POOL_DIM + 2 * CONV_DIM]
    gate_c = proj[:, o + POOL_DIM + 2 * CONV_DIM:o + POOL_DIM + 3 * CONV_DIM]
    pcz_ref[:, 0:POOL_DIM] = u_pool
    pcz_ref[:, POOL_DIM:POOL_DIM + CONV_DIM] = gate_c * u_conv
    pcz_ref[:, POOL_DIM + CONV_DIM:PCZ_COLS] = gate_b


def _proj(x, mod, g, w, layer, ones, gq, gk, seq_len, kv_dtype):
    t = x.shape[0]
    tm = PROJ_TILE
    if mod.shape[0] == 1:
        mod_map = lambda i: (0, 0, 0)
    else:
        tiles_per_group = seq_len // tm
        mod_map = lambda i: (i // tiles_per_group, 0, 0)
    row_spec = lambda c: pl.BlockSpec((tm, c), lambda i: (i, 0))
    return pl.pallas_call(
        _proj_kernel,
        grid=(t // tm,),
        in_specs=[row_spec(D_MODEL),
                  pl.BlockSpec((1, N_MOD, D_MODEL), mod_map),
                  _const_spec((1, D_MODEL)),
                  _layer_spec((D_MODEL, PROJ_COLS + MERGE_COLS), layer),
                  _const_spec((ATTN_DIM, ATTN_DIM)),
                  _const_spec((1, ATTN_DIM)),
                  _const_spec((1, ATTN_DIM))],
        out_specs=[row_spec(ATTN_DIM), row_spec(ATTN_DIM), row_spec(ATTN_DIM), row_spec(PCZ_COLS)],
        out_shape=[jax.ShapeDtypeStruct((t, ATTN_DIM), BF16),
                   jax.ShapeDtypeStruct((t, ATTN_DIM), kv_dtype),
                   jax.ShapeDtypeStruct((t, ATTN_DIM), kv_dtype),
                   jax.ShapeDtypeStruct((t, PCZ_COLS), F32)],
        compiler_params=_params(1),
        name="proj",
    )(x, mod, g, w, ones, gq, gk)


def _first_head_lanes():
    return lax.broadcasted_iota(jnp.int32, (1, LANES), 1) < HEAD_DIM


def _ctx_attn_kernel(*refs, n_prev):
    q_ref, k_ref, v_ref = refs[:3]
    o_ref = refs[3 + 2 * n_prev]
    if n_prev:
        layers = [(refs[3 + 2 * d], refs[4 + 2 * d]) for d in range(n_prev)] + [(k_ref, v_ref)]
        for dst_ref, which in ((refs[-2], 0), (refs[-1], 1)):
            for d, kv in enumerate(layers):
                for t in range(ATTN_DIM // LANES):
                    tile = kv[which][:, t * LANES:(t + 1) * LANES]
                    dst_ref[0, d, 2 * t] = tile[:, :HEAD_DIM]
                    dst_ref[0, d, 2 * t + 1] = tile[:, HEAD_DIM:]
    first = _first_head_lanes()
    for t in range(ATTN_DIM // LANES):
        sl = slice(t * LANES, (t + 1) * LANES)
        qt = q_ref[:, sl]
        kt = k_ref[:, sl].astype(BF16)
        vt = v_ref[:, sl].astype(BF16)
        outs = []
        for half in range(2):
            keep = first if half == 0 else jnp.logical_not(first)
            s = _dot_nt(jnp.where(keep, qt, jnp.zeros_like(qt)), kt)
            e = jnp.exp(s - jnp.max(s, axis=-1, keepdims=True))
            den = jnp.sum(e, axis=-1, keepdims=True)
            outs.append(_dot(e.astype(BF16), vt) / den)
        o_ref[:, sl] = jnp.where(first, outs[0], outs[1]).astype(o_ref.dtype)


def _ctx_attention(q, k, v, seq_len, prev_kv=()):
    t = q.shape[0]
    n_seq = t // seq_len
    n_prev = len(prev_kv)
    spec = pl.BlockSpec((seq_len, ATTN_DIM), lambda i: (i, 0))
    out_specs = [spec]
    out_shape = [jax.ShapeDtypeStruct((t, ATTN_DIM), BF16)]
    if n_prev:
        heads_shape = (n_seq, n_prev + 1, N_HEADS, seq_len, HEAD_DIM)
        heads_spec = pl.BlockSpec((1,) + heads_shape[1:], lambda i: (i, 0, 0, 0, 0))
        out_specs += [heads_spec, heads_spec]
        out_shape += [jax.ShapeDtypeStruct(heads_shape, k.dtype)] * 2
    return pl.pallas_call(
        functools.partial(_ctx_attn_kernel, n_prev=n_prev),
        grid=(n_seq,),
        in_specs=[spec] * (3 + 2 * n_prev),
        out_specs=out_specs,
        out_shape=out_shape,
        compiler_params=_params(1),
        name="ctx_attention",
    )(q, k, v, *[z for kv in prev_kv for z in kv])


def _row_max_lanes(s):
    m = s[:, :LANES]
    for p in range(1, s.shape[1] // LANES):
        m = jnp.maximum(m, s[:, p * LANES:(p + 1) * LANES])
    return jnp.broadcast_to(jnp.max(m, axis=-1, keepdims=True), m.shape)


def _exp_shifted(s, m):
    return jnp.concatenate([jnp.exp(s[:, p * LANES:(p + 1) * LANES] - m)
                            for p in range(s.shape[1] // LANES)], axis=1)


def _na_kernel(q_ref, k_ref, v_ref, kc_ref, vc_ref, bias_ref, o_ref,
               qs_ref, oc_ref, mc_ref, lc_ref, *, n_rows):
    first = _first_head_lanes()
    win_keys = WIN_ROWS * GRID_W
    pair_rows = 2 * GRID_W
    n_pairs = ATTN_DIM // LANES
    j = pl.program_id(1)

    past = kc_ref.shape[1]
    ones_past = jnp.ones((past, LANES), BF16)
    for t in range(n_pairs):
        sl = slice(t * LANES, (t + 1) * LANES)
        for r in range(NA_ROWS):
            qt = q_ref[0, r * GRID_W:(r + 1) * GRID_W, sl]
            zero = jnp.zeros_like(qt)
            qs_ref[t, r * pair_rows:r * pair_rows + GRID_W, :] = jnp.where(first, qt, zero)
            qs_ref[t, r * pair_rows + GRID_W:(r + 1) * pair_rows, :] = jnp.where(first, zero, qt)
        kct = kc_ref[0, :, sl]
        vc_aug = jnp.concatenate([vc_ref[0, :, sl], ones_past], axis=1)
        for c in range(qs_ref.shape[1] // NA_CTX_CHUNK):
            rows = slice(c * NA_CTX_CHUNK, (c + 1) * NA_CTX_CHUNK)
            s = _dot_nt(qs_ref[t, rows, :], kct)
            m = _row_max_lanes(s)
            pv = _dot(_exp_shifted(s, m).astype(BF16), vc_aug)
            mc_ref[t, rows, :] = m
            oc_ref[t, rows, :] = pv[:, :LANES]
            lc_ref[t, rows, :] = pv[:, LANES:]

    ones_win = jnp.ones((win_keys, LANES), BF16)

    def row(r, carry):
        i = j * NA_ROWS + r
        rs = jnp.clip(i - WIN_ROWS // 2, 0, n_rows - WIN_ROWS)
        off0 = rs - i + (WIN_ROWS - 1)
        kstart = pl.multiple_of(rs * GRID_W, GRID_W)
        qstart = pl.multiple_of(r * GRID_W, GRID_W)
        rows = pl.ds(pl.multiple_of(r * pair_rows, pair_rows), pair_rows)
        for t in range(n_pairs):
            sl = slice(t * LANES, (t + 1) * LANES)
            kt = k_ref[0, pl.ds(kstart, win_keys), sl]
            v_aug = jnp.concatenate([v_ref[0, pl.ds(kstart, win_keys), sl], ones_win], axis=1)
            bias = jnp.concatenate([bias_ref[t, off0 + 2 * p] for p in range(WIN_ROWS // 2)], axis=1)
            s = _dot_nt(qs_ref[t, rows, :], kt) + bias
            m_ctx = mc_ref[t, rows, :]
            m = jnp.maximum(_row_max_lanes(s), m_ctx)
            w_ctx = jnp.exp(m_ctx - m)
            pv = _dot(_exp_shifted(s, m).astype(BF16), v_aug)
            o = ((pv[:, :LANES] + oc_ref[t, rows, :] * w_ctx)
                 / (pv[:, LANES:] + lc_ref[t, rows, :] * w_ctx))
            o_ref[0, pl.ds(qstart, GRID_W), sl] = jnp.where(first, o[:GRID_W], o[GRID_W:]).astype(o_ref.dtype)
        return carry

    lax.fori_loop(0, NA_ROWS, row, 0, unroll=2)


def _na_attention(q, k, v, kc, vc, bias):
    b, t, _ = q.shape
    past = kc.shape[1]
    n_rows = t // GRID_W
    n_pairs = ATTN_DIM // LANES
    stacked = 2 * NA_ROWS * GRID_W
    q_spec = pl.BlockSpec((1, NA_ROWS * GRID_W, ATTN_DIM), lambda bi, j: (bi, j, 0))
    img_spec = pl.BlockSpec((1, t, ATTN_DIM), lambda bi, j: (bi, 0, 0))
    past_spec = pl.BlockSpec((1, past, ATTN_DIM), lambda bi, j: (bi, 0, 0))
    return pl.pallas_call(
        functools.partial(_na_kernel, n_rows=n_rows),
        grid=(b, n_rows // NA_ROWS),
        in_specs=[q_spec, img_spec, img_spec, past_spec, past_spec, _const_spec(bias.shape)],
        out_specs=q_spec,
        out_shape=jax.ShapeDtypeStruct((b, t, ATTN_DIM), BF16),
        scratch_shapes=[pltpu.VMEM((n_pairs, stacked, LANES), BF16),
                        pltpu.VMEM((n_pairs, stacked, LANES), F32),
                        pltpu.VMEM((n_pairs, stacked, LANES), F32),
                        pltpu.VMEM((n_pairs, stacked, LANES), F32)],
        compiler_params=_params(2),
        name="na_attention",
    )(q, k, v, kc, vc, bias)


def _bias_kernel(r_ref, sel_ref, mask_ref, o_ref):
    r = r_ref[...]
    hi = r.astype(BF16)
    r1 = r - hi.astype(F32)
    mid = r1.astype(BF16)
    lo = (r1 - mid.astype(F32)).astype(BF16)
    sel = sel_ref[...]
    o_ref[...] = _dot(hi, sel) + _dot(mid, sel) + _dot(lo, sel) + mask_ref[...]


def _na_bias_tables(rpb):
    n_rel = 2 * WIN_COLS - 1
    qc = np.arange(GRID_W)[:, None]
    kc = np.arange(GRID_W)[None, :]
    start = np.clip(qc - WIN_COLS // 2, 0, GRID_W - WIN_COLS)
    inside = ((kc >= start) & (kc < start + WIN_COLS)).reshape(-1)
    rel = (kc - qc + WIN_COLS - 1).reshape(-1)
    sel = (np.arange(LANES)[:, None] == rel[None, :]) & inside[None, :]
    mask = np.where(inside, 0.0, MASK_VALUE)[None, :]
    n_off = 2 * WIN_ROWS - 1
    rows = DEPTH * N_HEADS * n_off
    table = jnp.pad(rpb.reshape(rows, n_rel), ((0, 0), (0, LANES - n_rel)))
    dense = pl.pallas_call(
        _bias_kernel,
        out_shape=jax.ShapeDtypeStruct((rows, GRID_W * GRID_W), F32),
        compiler_params=pltpu.CompilerParams(vmem_limit_bytes=VMEM_LIMIT),
        name="na_bias",
    )(table, jnp.asarray(sel, dtype=BF16), jnp.asarray(mask, dtype=F32))
    dense = dense.reshape(DEPTH, N_HEADS // 2, 2, n_off, GRID_W, GRID_W)
    two_rows = jnp.concatenate([dense[:, :, :, :-1], dense[:, :, :, 1:]], axis=-1)
    return two_rows.transpose(0, 1, 3, 2, 4, 5).reshape(
        DEPTH, N_HEADS // 2, n_off - 1, 2 * GRID_W, 2 * GRID_W)


def _merge_kernel(x_ref, m_ref, g_ref, wm_ref, a_ref, pcz_ref, prev_ref, next_ref,
                  wpool_ref, pscale_ref, wconv_ref, bconv_ref, wa_ref, wp_ref, wc_ref, wo_ref,
                  o_ref, buf_ref, *, seq_len):
    tm = x_ref.shape[1]
    j = pl.program_id(1)
    n_tiles = pl.num_programs(1)

    buf_ref[HALO:HALO + tm, :] = pcz_ref[0]
    buf_ref[0:HALO, :] = jnp.where(j > 0, prev_ref[0], 0.0)
    buf_ref[HALO + tm:HALO + tm + HALO, :] = jnp.where(j < n_tiles - 1, next_ref[0], 0.0)

    def shifted(d, cols):
        return buf_ref[HALO + d:HALO + d + tm, cols]

    lane = lax.broadcasted_iota(jnp.int32, (1, LANES), 1)
    upper = lane >= POOL_GROUP_DIM
    pos = j * tm + lax.broadcasted_iota(jnp.int32, (tm, 1), 0)
    diffs = []
    for p in range(POOL_DIM // LANES):
        cols = slice(p * LANES, (p + 1) * LANES)
        w_lo, w_hi = POOL_WINDOWS[2 * p], POOL_WINDOWS[2 * p + 1]
        centre = shifted(0, cols)
        common = centre
        for d in range(-(w_lo // 2), w_lo // 2):
            if d != 0:
                common = common + shifted(d, cols)
        extra = None
        for d in list(range(-(w_hi // 2), -(w_lo // 2))) + list(range(w_lo // 2, w_hi // 2)):
            extra = shifted(d, cols) if extra is None else extra + shifted(d, cols)
        total = common + jnp.where(upper, extra, 0.0)
        half = jnp.where(upper, w_hi // 2, w_lo // 2)
        count = jnp.minimum(pos + half, seq_len) - jnp.maximum(pos - half, 0)
        diffs.append(total / count.astype(F32) - centre)
    diff = jnp.concatenate(diffs, axis=1).astype(BF16)
    pooled = _dot(diff, wpool_ref[...]) * pscale_ref[...]

    zc = slice(POOL_DIM, POOL_DIM + CONV_DIM)
    conv = (wconv_ref[0:1, :] * shifted(-1, zc) + wconv_ref[1:2, :] * shifted(0, zc)
            + wconv_ref[2:3, :] * shifted(1, zc) + bconv_ref[...])
    convd = shifted(0, slice(POOL_DIM + CONV_DIM, PCZ_COLS)) * conv

    x = x_ref[0]
    n = _norm_mod(x, g_ref[...], m_ref[0, 3:4, :], m_ref[0, 4:5, :]).astype(BF16)
    gates = jax.nn.sigmoid(_dot(n, wm_ref[:, PROJ_COLS:]))
    d = D_MODEL
    merged = (gates[:, 0:d] * _dot(a_ref[0], wa_ref[...])
              + gates[:, d:2 * d] * _dot(pooled.astype(BF16), wp_ref[...])
              + gates[:, 2 * d:3 * d] * _dot(convd.astype(BF16), wc_ref[...]))
    o_ref[0] = x + m_ref[0, 5:6, :] * _dot(merged.astype(BF16), wo_ref[...])


def _merge(x, mod, g, wm, layer, a, pcz, wpool, pscale, wconv, bconv, wa, wp, wc, wo, tm):
    b, seq_len, _ = x.shape
    n_tiles = seq_len // tm
    halo_per_tile = tm // HALO
    n_halo = seq_len // HALO
    mod_map = (lambda bi, j: (0, 0, 0)) if mod.shape[0] == 1 else (lambda bi, j: (bi, 0, 0))
    tile = lambda c: pl.BlockSpec((1, tm, c), lambda bi, j: (bi, j, 0))
    prev_spec = pl.BlockSpec((1, HALO, PCZ_COLS),
                             lambda bi, j: (bi, jnp.maximum(j * halo_per_tile - 1, 0), 0))
    next_spec = pl.BlockSpec((1, HALO, PCZ_COLS),
                             lambda bi, j: (bi, jnp.minimum((j + 1) * halo_per_tile, n_halo - 1), 0))
    return pl.pallas_call(
        functools.partial(_merge_kernel, seq_len=seq_len),
        grid=(b, n_tiles),
        in_specs=[tile(D_MODEL),
                  pl.BlockSpec((1, N_MOD, D_MODEL), mod_map),
                  _const_spec((1, D_MODEL)),
                  _layer_spec((D_MODEL, PROJ_COLS + MERGE_COLS), layer),
                  tile(ATTN_DIM), tile(PCZ_COLS), prev_spec, next_spec,
                  _layer_spec((POOL_DIM, POOL_DIM), layer),
                  _const_spec((1, POOL_DIM)),
                  _const_spec((3, CONV_DIM)),
                  _const_spec((1, CONV_DIM)),
                  _layer_spec((ATTN_DIM, D_MODEL), layer),
                  _layer_spec((POOL_DIM, D_MODEL), layer),
                  _layer_spec((CONV_DIM, D_MODEL), layer),
                  _layer_spec((D_MODEL, D_MODEL), layer)],
        out_specs=tile(D_MODEL),
        out_shape=jax.ShapeDtypeStruct((b, seq_len, D_MODEL), F32),
        scratch_shapes=[pltpu.VMEM((tm + 2 * HALO, PCZ_COLS), F32)],
        compiler_params=_params(2),
        name="merge",
    )(x, mod, g, wm, a, pcz, pcz, pcz, wpool, pscale, wconv, bconv, wa, wp, wc, wo)


def _head_ones():
    blk = np.arange(ATTN_DIM) // HEAD_DIM
    return jnp.asarray(blk[:, None] == blk[None, :], dtype=BF16)


def _layer(x, mod, sw, lw, layer, attend, merge_tile, kv_dtype):
    b, seq_len, _ = x.shape
    t = b * seq_len
    x2 = x.reshape(t, D_MODEL)
    x2 = _ffn(x2, mod, lw['g_ffn1'], sw['w1g'], sw['w1u'], sw['w1d'], layer, 0, seq_len)
    q, k, v, pcz = _proj(x2, mod, lw['g_mix'], sw['w_in'], layer, sw['ones'], lw['g_q'], lw['g_k'],
                         seq_len, kv_dtype)
    a, extras = attend(q, k, v)
    x3 = _merge(x2.reshape(b, seq_len, D_MODEL), mod, lw['g_mix'], sw['w_in'], layer,
                a.reshape(b, seq_len, ATTN_DIM), pcz.reshape(b, seq_len, PCZ_COLS),
                sw['w_pool'], lw['pool_scale'], lw['w_conv'], lw['b_conv'],
                sw['w_br_attn'], sw['w_br_pool'], sw['w_br_conv'], sw['w_out'], merge_tile)
    x2 = _ffn(x3.reshape(t, D_MODEL), mod, lw['g_ffn2'], sw['w2g'], sw['w2u'], sw['w2d'], layer, 6,
              seq_len)
    return x2.reshape(b, seq_len, D_MODEL), k, v, extras


def kernel(x_prompt, x_sample, cache_k, cache_v, c, c_ctx, w_mod, b_mod, g_ffn1, w_ffn1_gate, w_ffn1_up, w_ffn1_down, g_mix, w_in, g_q, g_k, rpb, w_pool, pool_scale, w_conv, b_conv, w_br_attn, w_br_pool, w_br_conv, w_out, g_ffn2, w_ffn2_gate, w_ffn2_up, w_ffn2_down):
    batch, seq, _ = x_prompt.shape
    dec_batch, dec_seq, _ = x_sample.shape
    past = cache_k.shape[3]
    assert dec_batch + 1 <= SUBLANES

    cvec = jnp.zeros((SUBLANES, D_MODEL), F32).at[0].set(c_ctx).at[1:1 + dec_batch].set(c)
    mod = _mod_table(cvec, w_mod, b_mod).reshape(DEPTH, SUBLANES, N_MOD, D_MODEL)

    na_bias = _na_bias_tables(rpb)
    group_eye = jnp.eye(POOL_GROUPS, dtype=F32)
    sw = {
        'w1g': w_ffn1_gate.astype(BF16), 'w1u': w_ffn1_up.astype(BF16), 'w1d': w_ffn1_down.astype(BF16),
        'w2g': w_ffn2_gate.astype(BF16), 'w2u': w_ffn2_up.astype(BF16), 'w2d': w_ffn2_down.astype(BF16),
        'w_in': w_in.astype(BF16), 'ones': _head_ones(),
        'w_pool': (group_eye[None, :, None, :, None] * w_pool[:, :, :, None, :]
                   ).reshape(DEPTH, POOL_DIM, POOL_DIM).astype(BF16),
        'w_br_attn': w_br_attn.astype(BF16), 'w_br_pool': w_br_pool.astype(BF16),
        'w_br_conv': w_br_conv.astype(BF16), 'w_out': w_out.astype(BF16),
    }
    from_heads = lambda z: z.transpose(0, 2, 1, 3).reshape(dec_batch, past, ATTN_DIM).astype(BF16)
    xp, xs = x_prompt, x_sample
    ctx_kv = []
    heads = None
    for l in range(DEPTH):
        row = lambda p: p[l].reshape(1, -1)
        lw = {
            'g_ffn1': row(g_ffn1), 'g_mix': row(g_mix), 'g_ffn2': row(g_ffn2),
            'g_q': jnp.tile(g_q[l], N_HEADS).reshape(1, ATTN_DIM),
            'g_k': jnp.tile(g_k[l], N_HEADS).reshape(1, ATTN_DIM),
            'pool_scale': row(pool_scale), 'w_conv': w_conv[l], 'b_conv': row(b_conv),
        }
        emit_heads = l == DEPTH - 1

        def attend_ctx(q, k, v):
            res = _ctx_attention(q, k, v, seq, prev_kv=tuple(ctx_kv) if emit_heads else ())
            return res[0], tuple(res[1:])

        xp, kp, vp, extras = _layer(xp, mod[l, 0:1], sw, lw, l, attend_ctx, seq, F32)
        ctx_kv.append((kp, vp))
        if emit_heads:
            heads = extras
        kc, vc = from_heads(cache_k[:, l]), from_heads(cache_v[:, l])
        bias = na_bias[l]

        def attend_lat(q, k, v):
            r3 = lambda z: z.reshape(dec_batch, dec_seq, ATTN_DIM)
            a = _na_attention(r3(q), r3(k), r3(v), kc, vc, bias)
            return a.reshape(dec_batch * dec_seq, ATTN_DIM), ()

        xs, _, _, _ = _layer(xs, mod[l, 1:1 + dec_batch], sw, lw, l, attend_lat, 512, BF16)
    new_k, new_v = heads
    return (xp, xs, new_k, new_v)
```

```python
import functools

import numpy as np
import jax
import jax.numpy as jnp
from jax import lax
from jax.experimental import pallas as pl
from jax.experimental.pallas import tpu as pltpu

D_MODEL = 1024
DEPTH = 2
GRID_W = 64
WIN_ROWS = 8
WIN_COLS = 16
N_HEADS = 8
HEAD_DIM = 64
ATTN_DIM = N_HEADS * HEAD_DIM
POOL_GROUPS = 4
POOL_GROUP_DIM = 64
POOL_DIM = POOL_GROUPS * POOL_GROUP_DIM
POOL_WINDOWS = (2, 4, 8, 16)
CONV_DIM = 256
FFN_DIM = 2816
N_MOD = 9
EPS = 1e-6
PROJ_COLS = 3 * ATTN_DIM + POOL_DIM + 3 * CONV_DIM
MERGE_COLS = 3 * D_MODEL
PCZ_COLS = POOL_DIM + 2 * CONV_DIM

LANES = 128
SUBLANES = 8
HALO = SUBLANES
MASK_VALUE = -1e30
VMEM_LIMIT = 56 * 1024 * 1024

FFN_TILE = 1024
FFN_SUBTILES = 4
MXU_DIM = 256
FFN_SPLITS = (0, (FFN_DIM // MXU_DIM // 2) * MXU_DIM, FFN_DIM)
PROJ_TILE = 1024
PROJ_SUBTILES = 4
MERGE_SUB_ROWS = 256
MERGE_TILE_LATENT = 1024
NA_ROWS = 8
NA_CTX_CHUNK = 256

F32 = jnp.float32
BF16 = jnp.bfloat16


def _dot(a, b):
    return jnp.dot(a, b, preferred_element_type=F32)


def _dot_nt(a, b):
    return lax.dot_general(a, b, (((1,), (1,)), ((), ())), preferred_element_type=F32)


def _split_bf16(x):
    hi = x.astype(BF16)
    lo = (x - hi.astype(F32)).astype(BF16)
    return hi, lo


def _sigmoid(x):
    return 0.5 * jnp.tanh(0.5 * x) + 0.5


def _norm_mod(x, g, shift, scale):
    ms = jnp.mean(x * x, axis=-1, keepdims=True)
    y = x * lax.rsqrt(ms + EPS) * g
    return y * (1.0 + scale) + shift


def _const_spec(shape):
    nd = len(shape)
    return pl.BlockSpec(shape, lambda *_: (0,) * nd, pipeline_mode=pl.Buffered(1))


def _layer_spec(shape, layer):
    nd = len(shape)
    return pl.BlockSpec((None,) + tuple(shape), lambda *_: (layer,) + (0,) * nd,
                        pipeline_mode=pl.Buffered(1))


def _params(n_axes):
    return pltpu.CompilerParams(dimension_semantics=("arbitrary",) * n_axes,
                                vmem_limit_bytes=VMEM_LIMIT)


def _mod_kernel(c_ref, w_ref, b_ref, o_ref):
    c = c_ref[...]
    a = c * jax.nn.sigmoid(c)
    a_hi, a_lo = _split_bf16(a)
    w_hi, w_lo = _split_bf16(w_ref[0])
    acc = _dot(a_hi, w_hi) + _dot(a_lo, w_hi) + _dot(a_hi, w_lo)
    o_ref[0] = acc + b_ref[0]


def _mod_table(cvec, w_mod, b_mod):
    nb = D_MODEL
    cols = N_MOD * D_MODEL
    return pl.pallas_call(
        _mod_kernel,
        grid=(DEPTH, cols // nb),
        in_specs=[pl.BlockSpec((SUBLANES, D_MODEL), lambda l, n: (0, 0)),
                  pl.BlockSpec((1, D_MODEL, nb), lambda l, n: (l, 0, n)),
                  pl.BlockSpec((1, 1, nb), lambda l, n: (l, 0, n))],
        out_specs=pl.BlockSpec((1, SUBLANES, nb), lambda l, n: (l, 0, n)),
        out_shape=jax.ShapeDtypeStruct((DEPTH, SUBLANES, cols), F32),
        compiler_params=_params(2),
        name="adaln_table",
    )(cvec, w_mod, b_mod.reshape(DEPTH, 1, cols))


def _ffn_kernel(x_ref, m_ref, g_ref, wg_ref, wu_ref, wd_ref, o_ref, *, mi):
    shift = m_ref[0, mi:mi + 1, :]
    scale = m_ref[0, mi + 1:mi + 2, :]
    gate = m_ref[0, mi + 2:mi + 3, :]
    sub = x_ref.shape[0] // FFN_SUBTILES
    for r in range(FFN_SUBTILES):
        rows = slice(r * sub, (r + 1) * sub)
        x = x_ref[rows, :]
        h = _norm_mod(x, g_ref[...], shift, scale).astype(BF16)
        acc = None
        for lo, hi in zip(FFN_SPLITS[:-1], FFN_SPLITS[1:]):
            sl = slice(lo, hi)
            a = _dot(h, wg_ref[:, sl])
            b = _dot(h, wu_ref[:, sl])
            s = (a * _sigmoid(a) * b).astype(BF16)
            y = _dot(s, wd_ref[sl, :])
            acc = y if acc is None else acc + y
        o_ref[rows, :] = x + (0.5 * gate) * acc


def _ffn(x, mod, g, wg, wu, wd, layer, mi, seq_len):
    t = x.shape[0]
    tm = FFN_TILE
    if mod.shape[0] == 1:
        mod_map = lambda i: (0, 0, 0)
    else:
        tiles_per_group = seq_len // tm
        mod_map = lambda i: (i // tiles_per_group, 0, 0)
    return pl.pallas_call(
        functools.partial(_ffn_kernel, mi=mi),
        grid=(t // tm,),
        in_specs=[pl.BlockSpec((tm, D_MODEL), lambda i: (i, 0)),
                  pl.BlockSpec((1, N_MOD, D_MODEL), mod_map),
                  _const_spec((1, D_MODEL)),
                  _layer_spec((D_MODEL, FFN_DIM), layer),
                  _layer_spec((D_MODEL, FFN_DIM), layer),
                  _layer_spec((FFN_DIM, D_MODEL), layer)],
        out_specs=pl.BlockSpec((tm, D_MODEL), lambda i: (i, 0)),
        out_shape=jax.ShapeDtypeStruct((t, D_MODEL), F32),
        compiler_params=_params(1),
        name="ffn",
    )(x, mod, g, wg, wu, wd)


def _proj_kernel(x_ref, m_ref, g_ref, w_ref, ones_ref, gq_ref, gk_ref,
                 q_ref, k_ref, v_ref, pcz_ref):
    ones = ones_ref[...]

    def head_norm(t, gain):
        ss = _dot((t * t).astype(BF16), ones)
        return t * lax.rsqrt(ss * (1.0 / HEAD_DIM) + EPS) * gain

    a = ATTN_DIM
    o = 3 * a
    sub = x_ref.shape[0] // PROJ_SUBTILES
    for r in range(PROJ_SUBTILES):
        rows = slice(r * sub, (r + 1) * sub)
        n = _norm_mod(x_ref[rows, :], g_ref[...], m_ref[0, 3:4, :], m_ref[0, 4:5, :]).astype(BF16)
        proj = _dot(n, w_ref[:, :PROJ_COLS])
        q = head_norm(proj[:, 0:a], gq_ref[...])
        q_ref[rows, :] = (q * (HEAD_DIM ** -0.5)).astype(q_ref.dtype)
        k_ref[rows, :] = head_norm(proj[:, a:2 * a], gk_ref[...]).astype(k_ref.dtype)
        v_ref[rows, :] = proj[:, 2 * a:3 * a].astype(v_ref.dtype)
        u_pool = proj[:, o:o + POOL_DIM]
        u_conv = proj[:, o + POOL_DIM:o + POOL_DIM + CONV_DIM]
        gate_b = proj[:, o + POOL_DIM + CONV_DIM:o + POOL_DIM + 2 * CONV_DIM]
        gate_c = proj[:, o + POOL_DIM + 2 * CONV_DIM:o + POOL_DIM + 3 * CONV_DIM]
        pcz_ref[rows, 0:POOL_DIM] = u_pool
        pcz_ref[rows, POOL_DIM:POOL_DIM + CONV_DIM] = gate_c * u_conv
        pcz_ref[rows, POOL_DIM + CONV_DIM:PCZ_COLS] = gate_b


def _proj(x, mod, g, w, layer, ones, gq, gk, seq_len, kv_dtype):
    t = x.shape[0]
    tm = PROJ_TILE
    if mod.shape[0] == 1:
        mod_map = lambda i: (0, 0, 0)
    else:
        tiles_per_group = seq_len // tm
        mod_map = lambda i: (i // tiles_per_group, 0, 0)
    row_spec = lambda c: pl.BlockSpec((tm, c), lambda i: (i, 0))
    return pl.pallas_call(
        _proj_kernel,
        grid=(t // tm,),
        in_specs=[row_spec(D_MODEL),
                  pl.BlockSpec((1, N_MOD, D_MODEL), mod_map),
                  _const_spec((1, D_MODEL)),
                  _layer_spec((D_MODEL, PROJ_COLS + MERGE_COLS), layer),
                  _const_spec((ATTN_DIM, ATTN_DIM)),
                  _const_spec((1, ATTN_DIM)),
                  _const_spec((1, ATTN_DIM))],
        out_specs=[row_spec(ATTN_DIM), row_spec(ATTN_DIM), row_spec(ATTN_DIM), row_spec(PCZ_COLS)],
        out_shape=[jax.ShapeDtypeStruct((t, ATTN_DIM), BF16),
                   jax.ShapeDtypeStruct((t, ATTN_DIM), kv_dtype),
                   jax.ShapeDtypeStruct((t, ATTN_DIM), kv_dtype),
                   jax.ShapeDtypeStruct((t, PCZ_COLS), F32)],
        compiler_params=_params(1),
        name="proj",
    )(x, mod, g, w, ones, gq, gk)


def _first_head_lanes():
    return lax.broadcasted_iota(jnp.int32, (1, LANES), 1) < HEAD_DIM


def _ctx_attn_kernel(*refs, n_prev):
    q_ref, k_ref, v_ref = refs[:3]
    o_ref = refs[3 + 2 * n_prev]
    if n_prev:
        layers = [(refs[3 + 2 * d], refs[4 + 2 * d]) for d in range(n_prev)] + [(k_ref, v_ref)]
        for dst_ref, which in ((refs[-2], 0), (refs[-1], 1)):
            for d, kv in enumerate(layers):
                for t in range(ATTN_DIM // LANES):
                    tile = kv[which][:, t * LANES:(t + 1) * LANES]
                    dst_ref[0, d, 2 * t] = tile[:, :HEAD_DIM]
                    dst_ref[0, d, 2 * t + 1] = tile[:, HEAD_DIM:]
    first = _first_head_lanes()
    for t in range(ATTN_DIM // LANES):
        sl = slice(t * LANES, (t + 1) * LANES)
        qt = q_ref[:, sl]
        kt = k_ref[:, sl].astype(BF16)
        vt = v_ref[:, sl].astype(BF16)
        outs = []
        for half in range(2):
            keep = first if half == 0 else jnp.logical_not(first)
            s = _dot_nt(jnp.where(keep, qt, jnp.zeros_like(qt)), kt)
            e = jnp.exp(s - jnp.max(s, axis=-1, keepdims=True))
            den = jnp.sum(e, axis=-1, keepdims=True)
            outs.append(_dot(e.astype(BF16), vt) / den)
        o_ref[:, sl] = jnp.where(first, outs[0], outs[1]).astype(o_ref.dtype)


def _ctx_attention(q, k, v, seq_len, prev_kv=()):
    t = q.shape[0]
    n_seq = t // seq_len
    n_prev = len(prev_kv)
    spec = pl.BlockSpec((seq_len, ATTN_DIM), lambda i: (i, 0))
    out_specs = [spec]
    out_shape = [jax.ShapeDtypeStruct((t, ATTN_DIM), BF16)]
    if n_prev:
        heads_shape = (n_seq, n_prev + 1, N_HEADS, seq_len, HEAD_DIM)
        heads_spec = pl.BlockSpec((1,) + heads_shape[1:], lambda i: (i, 0, 0, 0, 0))
        out_specs += [heads_spec, heads_spec]
        out_shape += [jax.ShapeDtypeStruct(heads_shape, k.dtype)] * 2
    return pl.pallas_call(
        functools.partial(_ctx_attn_kernel, n_prev=n_prev),
        grid=(n_seq,),
        in_specs=[spec] * (3 + 2 * n_prev),
        out_specs=out_specs,
        out_shape=out_shape,
        compiler_params=_params(1),
        name="ctx_attention",
    )(q, k, v, *[z for kv in prev_kv for z in kv])


def _row_max_lanes(s):
    m = s[:, :LANES]
    for p in range(1, s.shape[1] // LANES):
        m = jnp.maximum(m, s[:, p * LANES:(p + 1) * LANES])
    return jnp.broadcast_to(jnp.max(m, axis=-1, keepdims=True), m.shape)


def _exp_shifted(s, m):
    return jnp.concatenate([jnp.exp(s[:, p * LANES:(p + 1) * LANES] - m)
                            for p in range(s.shape[1] // LANES)], axis=1)


def _na_kernel(q_ref, k_ref, v_ref, kc_ref, vc_ref, bias_ref, o_ref,
               qs_ref, oc_ref, mc_ref, lc_ref, *, n_rows):
    first = _first_head_lanes()
    win_keys = WIN_ROWS * GRID_W
    pair_rows = 2 * GRID_W
    n_pairs = ATTN_DIM // LANES
    j = pl.program_id(1)

    past = kc_ref.shape[1]
    ones_past = jnp.ones((past, LANES), BF16)
    for t in range(n_pairs):
        sl = slice(t * LANES, (t + 1) * LANES)
        for r in range(NA_ROWS):
            qt = q_ref[0, r * GRID_W:(r + 1) * GRID_W, sl]
            zero = jnp.zeros_like(qt)
            qs_ref[t, r * pair_rows:r * pair_rows + GRID_W, :] = jnp.where(first, qt, zero)
            qs_ref[t, r * pair_rows + GRID_W:(r + 1) * pair_rows, :] = jnp.where(first, zero, qt)
        kct = kc_ref[0, :, sl]
        vc_aug = jnp.concatenate([vc_ref[0, :, sl], ones_past], axis=1)
        for c in range(qs_ref.shape[1] // NA_CTX_CHUNK):
            rows = slice(c * NA_CTX_CHUNK, (c + 1) * NA_CTX_CHUNK)
            s = _dot_nt(qs_ref[t, rows, :], kct)
            m = _row_max_lanes(s)
            pv = _dot(_exp_shifted(s, m).astype(BF16), vc_aug)
            mc_ref[t, rows, :] = m
            oc_ref[t, rows, :] = pv[:, :LANES]
            lc_ref[t, rows, :] = pv[:, LANES:]

    ones_win = jnp.ones((win_keys, LANES), BF16)

    def row(r, carry):
        i = j * NA_ROWS + r
        rs = jnp.clip(i - WIN_ROWS // 2, 0, n_rows - WIN_ROWS)
        off0 = rs - i + (WIN_ROWS - 1)
        kstart = pl.multiple_of(rs * GRID_W, GRID_W)
        qstart = pl.multiple_of(r * GRID_W, GRID_W)
        rows = pl.ds(pl.multiple_of(r * pair_rows, pair_rows), pair_rows)
        for t in range(n_pairs):
            sl = slice(t * LANES, (t + 1) * LANES)
            kt = k_ref[0, pl.ds(kstart, win_keys), sl]
            v_aug = jnp.concatenate([v_ref[0, pl.ds(kstart, win_keys), sl], ones_win], axis=1)
            bias = jnp.concatenate([bias_ref[t, off0 + 2 * p] for p in range(WIN_ROWS // 2)], axis=1)
            s = _dot_nt(qs_ref[t, rows, :], kt) + bias
            m_ctx = mc_ref[t, rows, :]
            m = jnp.maximum(_row_max_lanes(s), m_ctx)
            w_ctx = jnp.exp(m_ctx - m)
            pv = _dot(_exp_shifted(s, m).astype(BF16), v_aug)
            o = ((pv[:, :LANES] + oc_ref[t, rows, :] * w_ctx)
                 / (pv[:, LANES:] + lc_ref[t, rows, :] * w_ctx))
            o_ref[0, pl.ds(qstart, GRID_W), sl] = jnp.where(first, o[:GRID_W], o[GRID_W:]).astype(o_ref.dtype)
        return carry

    lax.fori_loop(0, NA_ROWS, row, 0, unroll=2)


def _na_attention(q, k, v, kc, vc, bias):
    b, t, _ = q.shape
    past = kc.shape[1]
    n_rows = t // GRID_W
    n_pairs = ATTN_DIM // LANES
    stacked = 2 * NA_ROWS * GRID_W
    q_spec = pl.BlockSpec((1, NA_ROWS * GRID_W, ATTN_DIM), lambda bi, j: (bi, j, 0))
    img_spec = pl.BlockSpec((1, t, ATTN_DIM), lambda bi, j: (bi, 0, 0))
    past_spec = pl.BlockSpec((1, past, ATTN_DIM), lambda bi, j: (bi, 0, 0))
    return pl.pallas_call(
        functools.partial(_na_kernel, n_rows=n_rows),
        grid=(b, n_rows // NA_ROWS),
        in_specs=[q_spec, img_spec, img_spec, past_spec, past_spec, _const_spec(bias.shape)],
        out_specs=q_spec,
        out_shape=jax.ShapeDtypeStruct((b, t, ATTN_DIM), BF16),
        scratch_shapes=[pltpu.VMEM((n_pairs, stacked, LANES), BF16),
                        pltpu.VMEM((n_pairs, stacked, LANES), F32),
                        pltpu.VMEM((n_pairs, stacked, LANES), F32),
                        pltpu.VMEM((n_pairs, stacked, LANES), F32)],
        compiler_params=_params(2),
        name="na_attention",
    )(q, k, v, kc, vc, bias)


def _bias_kernel(r_ref, sel_ref, mask_ref, o_ref):
    r = r_ref[...]
    hi = r.astype(BF16)
    r1 = r - hi.astype(F32)
    mid = r1.astype(BF16)
    lo = (r1 - mid.astype(F32)).astype(BF16)
    sel = sel_ref[...]
    o_ref[...] = _dot(hi, sel) + _dot(mid, sel) + _dot(lo, sel) + mask_ref[...]


def _na_bias_tables(rpb):
    n_rel = 2 * WIN_COLS - 1
    qc = np.arange(GRID_W)[:, None]
    kc = np.arange(GRID_W)[None, :]
    start = np.clip(qc - WIN_COLS // 2, 0, GRID_W - WIN_COLS)
    inside = ((kc >= start) & (kc < start + WIN_COLS)).reshape(-1)
    rel = (kc - qc + WIN_COLS - 1).reshape(-1)
    sel = (np.arange(LANES)[:, None] == rel[None, :]) & inside[None, :]
    mask = np.where(inside, 0.0, MASK_VALUE)[None, :]
    n_off = 2 * WIN_ROWS - 1
    rows = DEPTH * N_HEADS * n_off
    table = jnp.pad(rpb.reshape(rows, n_rel), ((0, 0), (0, LANES - n_rel)))
    dense = pl.pallas_call(
        _bias_kernel,
        out_shape=jax.ShapeDtypeStruct((rows, GRID_W * GRID_W), F32),
        compiler_params=pltpu.CompilerParams(vmem_limit_bytes=VMEM_LIMIT),
        name="na_bias",
    )(table, jnp.asarray(sel, dtype=BF16), jnp.asarray(mask, dtype=F32))
    dense = dense.reshape(DEPTH, N_HEADS // 2, 2, n_off, GRID_W, GRID_W)
    two_rows = jnp.concatenate([dense[:, :, :, :-1], dense[:, :, :, 1:]], axis=-1)
    return two_rows.transpose(0, 1, 3, 2, 4, 5).reshape(
        DEPTH, N_HEADS // 2, n_off - 1, 2 * GRID_W, 2 * GRID_W)


def _merge_kernel(x_ref, m_ref, g_ref, wm_ref, a_ref, pcz_ref, prev_ref, next_ref,
                  wpool_ref, pscale_ref, wconv_ref, bconv_ref, wa_ref, wp_ref, wc_ref, wo_ref,
                  o_ref, buf_ref, *, seq_len):
    tm = x_ref.shape[1]
    j = pl.program_id(1)
    n_tiles = pl.num_programs(1)

    buf_ref[HALO:HALO + tm, :] = pcz_ref[0]
    buf_ref[0:HALO, :] = jnp.where(j > 0, prev_ref[0], 0.0)
    buf_ref[HALO + tm:HALO + tm + HALO, :] = jnp.where(j < n_tiles - 1, next_ref[0], 0.0)

    lane = lax.broadcasted_iota(jnp.int32, (1, LANES), 1)
    upper = lane >= POOL_GROUP_DIM
    sub = MERGE_SUB_ROWS
    for r in range(tm // sub):
        base = r * sub

        def shifted(d, cols):
            return buf_ref[HALO + base + d:HALO + base + d + sub, cols]

        pos = j * tm + base + lax.broadcasted_iota(jnp.int32, (sub, 1), 0)
        diffs = []
        for p in range(POOL_DIM // LANES):
            cols = slice(p * LANES, (p + 1) * LANES)
            w_lo, w_hi = POOL_WINDOWS[2 * p], POOL_WINDOWS[2 * p + 1]
            centre = shifted(0, cols)
            common = centre
            for d in range(-(w_lo // 2), w_lo // 2):
                if d != 0:
                    common = common + shifted(d, cols)
            extra = None
            for d in list(range(-(w_hi // 2), -(w_lo // 2))) + list(range(w_lo // 2, w_hi // 2)):
                extra = shifted(d, cols) if extra is None else extra + shifted(d, cols)
            total = common + jnp.where(upper, extra, 0.0)
            half = jnp.where(upper, w_hi // 2, w_lo // 2)
            count = jnp.minimum(pos + half, seq_len) - jnp.maximum(pos - half, 0)
            diffs.append(total / count.astype(F32) - centre)
        diff = jnp.concatenate(diffs, axis=1).astype(BF16)
        pooled = _dot(diff, wpool_ref[...]) * pscale_ref[...]

        zc = slice(POOL_DIM, POOL_DIM + CONV_DIM)
        conv = (wconv_ref[0:1, :] * shifted(-1, zc) + wconv_ref[1:2, :] * shifted(0, zc)
                + wconv_ref[2:3, :] * shifted(1, zc) + bconv_ref[...])
        convd = shifted(0, slice(POOL_DIM + CONV_DIM, PCZ_COLS)) * conv

        rows = slice(base, base + sub)
        x = x_ref[0, rows, :]
        n = _norm_mod(x, g_ref[...], m_ref[0, 3:4, :], m_ref[0, 4:5, :]).astype(BF16)
        merged = None
        branches = ((a_ref[0, rows, :], wa_ref), (pooled.astype(BF16), wp_ref),
                    (convd.astype(BF16), wc_ref))
        for i, (branch_in, w_ref) in enumerate(branches):
            lo = PROJ_COLS + i * D_MODEL
            gate = _sigmoid(_dot(n, wm_ref[:, lo:lo + D_MODEL]))
            term = gate * _dot(branch_in, w_ref[...])
            merged = term if merged is None else merged + term
        o_ref[0, rows, :] = x + m_ref[0, 5:6, :] * _dot(merged.astype(BF16), wo_ref[...])


def _merge(x, mod, g, wm, layer, a, pcz, wpool, pscale, wconv, bconv, wa, wp, wc, wo, tm):
    b, seq_len, _ = x.shape
    n_tiles = seq_len // tm
    halo_per_tile = tm // HALO
    n_halo = seq_len // HALO
    mod_map = (lambda bi, j: (0, 0, 0)) if mod.shape[0] == 1 else (lambda bi, j: (bi, 0, 0))
    tile = lambda c: pl.BlockSpec((1, tm, c), lambda bi, j: (bi, j, 0))
    prev_spec = pl.BlockSpec((1, HALO, PCZ_COLS),
                             lambda bi, j: (bi, jnp.maximum(j * halo_per_tile - 1, 0), 0))
    next_spec = pl.BlockSpec((1, HALO, PCZ_COLS),
                             lambda bi, j: (bi, jnp.minimum((j + 1) * halo_per_tile, n_halo - 1), 0))
    return pl.pallas_call(
        functools.partial(_merge_kernel, seq_len=seq_len),
        grid=(b, n_tiles),
        in_specs=[tile(D_MODEL),
                  pl.BlockSpec((1, N_MOD, D_MODEL), mod_map),
                  _const_spec((1, D_MODEL)),
                  _layer_spec((D_MODEL, PROJ_COLS + MERGE_COLS), layer),
                  tile(ATTN_DIM), tile(PCZ_COLS), prev_spec, next_spec,
                  _layer_spec((POOL_DIM, POOL_DIM), layer),
                  _const_spec((1, POOL_DIM)),
                  _const_spec((3, CONV_DIM)),
                  _const_spec((1, CONV_DIM)),
                  _layer_spec((ATTN_DIM, D_MODEL), layer),
                  _layer_spec((POOL_DIM, D_MODEL), layer),
                  _layer_spec((CONV_DIM, D_MODEL), layer),
                  _layer_spec((D_MODEL, D_MODEL), layer)],
        out_specs=tile(D_MODEL),
        out_shape=jax.ShapeDtypeStruct((b, seq_len, D_MODEL), F32),
        scratch_shapes=[pltpu.VMEM((tm + 2 * HALO, PCZ_COLS), F32)],
        compiler_params=_params(2),
        name="merge",
    )(x, mod, g, wm, a, pcz, pcz, pcz, wpool, pscale, wconv, bconv, wa, wp, wc, wo)


def _head_ones():
    blk = np.arange(ATTN_DIM) // HEAD_DIM
    return jnp.asarray(blk[:, None] == blk[None, :], dtype=BF16)


def _layer(x, mod, sw, lw, layer, attend, merge_tile, kv_dtype):
    b, seq_len, _ = x.shape
    t = b * seq_len
    x2 = x.reshape(t, D_MODEL)
    x2 = _ffn(x2, mod, lw['g_ffn1'], sw['w1g'], sw['w1u'], sw['w1d'], layer, 0, seq_len)
    q, k, v, pcz = _proj(x2, mod, lw['g_mix'], sw['w_in'], layer, sw['ones'], lw['g_q'], lw['g_k'],
                         seq_len, kv_dtype)
    a, extras = attend(q, k, v)
    x3 = _merge(x2.reshape(b, seq_len, D_MODEL), mod, lw['g_mix'], sw['w_in'], layer,
                a.reshape(b, seq_len, ATTN_DIM), pcz.reshape(b, seq_len, PCZ_COLS),
                sw['w_pool'], lw['pool_scale'], lw['w_conv'], lw['b_conv'],
                sw['w_br_attn'], sw['w_br_pool'], sw['w_br_conv'], sw['w_out'], merge_tile)
    x2 = _ffn(x3.reshape(t, D_MODEL), mod, lw['g_ffn2'], sw['w2g'], sw['w2u'], sw['w2d'], layer, 6,
              seq_len)
    return x2.reshape(b, seq_len, D_MODEL), k, v, extras


def kernel(x_prompt, x_sample, cache_k, cache_v, c, c_ctx, w_mod, b_mod, g_ffn1, w_ffn1_gate, w_ffn1_up, w_ffn1_down, g_mix, w_in, g_q, g_k, rpb, w_pool, pool_scale, w_conv, b_conv, w_br_attn, w_br_pool, w_br_conv, w_out, g_ffn2, w_ffn2_gate, w_ffn2_up, w_ffn2_down):
    batch, seq, _ = x_prompt.shape
    dec_batch, dec_seq, _ = x_sample.shape
    past = cache_k.shape[3]
    assert dec_batch + 1 <= SUBLANES

    cvec = jnp.zeros((SUBLANES, D_MODEL), F32).at[0].set(c_ctx).at[1:1 + dec_batch].set(c)
    mod = _mod_table(cvec, w_mod, b_mod).reshape(DEPTH, SUBLANES, N_MOD, D_MODEL)

    na_bias = _na_bias_tables(rpb)
    group_eye = jnp.eye(POOL_GROUPS, dtype=F32)
    sw = {
        'w1g': w_ffn1_gate.astype(BF16), 'w1u': w_ffn1_up.astype(BF16), 'w1d': w_ffn1_down.astype(BF16),
        'w2g': w_ffn2_gate.astype(BF16), 'w2u': w_ffn2_up.astype(BF16), 'w2d': w_ffn2_down.astype(BF16),
        'w_in': w_in.astype(BF16), 'ones': _head_ones(),
        'w_pool': (group_eye[None, :, None, :, None] * w_pool[:, :, :, None, :]
                   ).reshape(DEPTH, POOL_DIM, POOL_DIM).astype(BF16),
        'w_br_attn': w_br_attn.astype(BF16), 'w_br_pool': w_br_pool.astype(BF16),
        'w_br_conv': w_br_conv.astype(BF16), 'w_out': w_out.astype(BF16),
    }
    from_heads = lambda z: z.transpose(0, 2, 1, 3).reshape(dec_batch, past, ATTN_DIM).astype(BF16)
    xp, xs = x_prompt, x_sample
    ctx_kv = []
    heads = None
    for l in range(DEPTH):
        row = lambda p: p[l].reshape(1, -1)
        lw = {
            'g_ffn1': row(g_ffn1), 'g_mix': row(g_mix), 'g_ffn2': row(g_ffn2),
            'g_q': jnp.tile(g_q[l], N_HEADS).reshape(1, ATTN_DIM),
            'g_k': jnp.tile(g_k[l], N_HEADS).reshape(1, ATTN_DIM),
            'pool_scale': row(pool_scale), 'w_conv': w_conv[l], 'b_conv': row(b_conv),
        }
        emit_heads = l == DEPTH - 1

        def attend_ctx(q, k, v):
            res = _ctx_attention(q, k, v, seq, prev_kv=tuple(ctx_kv) if emit_heads else ())
            return res[0], tuple(res[1:])

        xp, kp, vp, extras = _layer(xp, mod[l, 0:1], sw, lw, l, attend_ctx, seq, F32)
        ctx_kv.append((kp, vp))
        if emit_heads:
            heads = extras
        kc, vc = from_heads(cache_k[:, l]), from_heads(cache_v[:, l])
        bias = na_bias[l]

        def attend_lat(q, k, v):
            r3 = lambda z: z.reshape(dec_batch, dec_seq, ATTN_DIM)
            a = _na_attention(r3(q), r3(k), r3(v), kc, vc, bias)
            return a.reshape(dec_batch * dec_seq, ATTN_DIM), ()

        xs, _, _, _ = _layer(xs, mod[l, 1:1 + dec_batch], sw, lw, l, attend_lat, MERGE_TILE_LATENT, BF16)
    new_k, new_v = heads
    return (xp, xs, new_k, new_v)
```

```python
import functools

import numpy as np
import jax
import jax.numpy as jnp
from jax import lax
from jax.experimental import pallas as pl
from jax.experimental.pallas import tpu as pltpu

D_MODEL = 1024
DEPTH = 2
GRID_W = 64
WIN_ROWS = 8
WIN_COLS = 16
N_HEADS = 8
HEAD_DIM = 64
ATTN_DIM = N_HEADS * HEAD_DIM
POOL_GROUPS = 4
POOL_GROUP_DIM = 64
POOL_DIM = POOL_GROUPS * POOL_GROUP_DIM
POOL_WINDOWS = (2, 4, 8, 16)
CONV_DIM = 256
FFN_DIM = 2816
N_MOD = 9
EPS = 1e-6
PROJ_COLS = 3 * ATTN_DIM + POOL_DIM + 3 * CONV_DIM
MERGE_COLS = 3 * D_MODEL
PCZ_COLS = POOL_DIM + 2 * CONV_DIM

LANES = 128
SUBLANES = 8
HALO = SUBLANES
MASK_VALUE = -1e30
VMEM_LIMIT = 56 * 1024 * 1024

FFN_TILE = 1024
FFN_SUBTILES = 4
MXU_DIM = 256
FFN_SPLITS = (0, (FFN_DIM // MXU_DIM // 2) * MXU_DIM, FFN_DIM)
PROJ_TILE = 512
PROJ_SUBTILES = 2
MERGE_SUB_ROWS = 256
MERGE_TILE_LATENT = 1024
NA_ROWS = 8
NA_CTX_CHUNK = 256

F32 = jnp.float32
BF16 = jnp.bfloat16


def _dot(a, b):
    return jnp.dot(a, b, preferred_element_type=F32)


def _dot_nt(a, b):
    return lax.dot_general(a, b, (((1,), (1,)), ((), ())), preferred_element_type=F32)


def _split_bf16(x):
    hi = x.astype(BF16)
    lo = (x - hi.astype(F32)).astype(BF16)
    return hi, lo


def _sigmoid(x):
    return 0.5 * jnp.tanh(0.5 * x) + 0.5


def _norm_mod(x, g, shift, scale):
    ms = jnp.mean(x * x, axis=-1, keepdims=True)
    y = x * lax.rsqrt(ms + EPS) * g
    return y * (1.0 + scale) + shift


def _const_spec(shape):
    nd = len(shape)
    return pl.BlockSpec(shape, lambda *_: (0,) * nd, pipeline_mode=pl.Buffered(1))


def _layer_spec(shape, layer):
    nd = len(shape)
    return pl.BlockSpec((None,) + tuple(shape), lambda *_: (layer,) + (0,) * nd,
                        pipeline_mode=pl.Buffered(1))


def _params(n_axes):
    return pltpu.CompilerParams(dimension_semantics=("arbitrary",) * n_axes,
                                vmem_limit_bytes=VMEM_LIMIT)


def _mod_kernel(c_ref, w_ref, b_ref, o_ref):
    c = c_ref[...]
    a = c * jax.nn.sigmoid(c)
    a_hi, a_lo = _split_bf16(a)
    w_hi, w_lo = _split_bf16(w_ref[0])
    acc = _dot(a_hi, w_hi) + _dot(a_lo, w_hi) + _dot(a_hi, w_lo)
    o_ref[0] = acc + b_ref[0]


def _mod_table(cvec, w_mod, b_mod):
    nb = D_MODEL
    cols = N_MOD * D_MODEL
    return pl.pallas_call(
        _mod_kernel,
        grid=(DEPTH, cols // nb),
        in_specs=[pl.BlockSpec((SUBLANES, D_MODEL), lambda l, n: (0, 0)),
                  pl.BlockSpec((1, D_MODEL, nb), lambda l, n: (l, 0, n)),
                  pl.BlockSpec((1, 1, nb), lambda l, n: (l, 0, n))],
        out_specs=pl.BlockSpec((1, SUBLANES, nb), lambda l, n: (l, 0, n)),
        out_shape=jax.ShapeDtypeStruct((DEPTH, SUBLANES, cols), F32),
        compiler_params=_params(2),
        name="adaln_table",
    )(cvec, w_mod, b_mod.reshape(DEPTH, 1, cols))


def _ffn_kernel(x_ref, m_ref, g_ref, wg_ref, wu_ref, wd_ref, o_ref, *, mi):
    shift = m_ref[0, mi:mi + 1, :]
    scale = m_ref[0, mi + 1:mi + 2, :]
    gate = m_ref[0, mi + 2:mi + 3, :]
    sub = x_ref.shape[0] // FFN_SUBTILES
    for r in range(FFN_SUBTILES):
        rows = slice(r * sub, (r + 1) * sub)
        x = x_ref[rows, :]
        h = _norm_mod(x, g_ref[...], shift, scale).astype(BF16)
        acc = None
        for lo, hi in zip(FFN_SPLITS[:-1], FFN_SPLITS[1:]):
            sl = slice(lo, hi)
            a = _dot(h, wg_ref[:, sl])
            b = _dot(h, wu_ref[:, sl])
            s = (a * _sigmoid(a) * b).astype(BF16)
            y = _dot(s, wd_ref[sl, :])
            acc = y if acc is None else acc + y
        o_ref[rows, :] = x + (0.5 * gate) * acc


def _ffn(x, mod, g, wg, wu, wd, layer, mi, seq_len):
    t = x.shape[0]
    tm = FFN_TILE
    if mod.shape[0] == 1:
        mod_map = lambda i: (0, 0, 0)
    else:
        tiles_per_group = seq_len // tm
        mod_map = lambda i: (i // tiles_per_group, 0, 0)
    return pl.pallas_call(
        functools.partial(_ffn_kernel, mi=mi),
        grid=(t // tm,),
        in_specs=[pl.BlockSpec((tm, D_MODEL), lambda i: (i, 0)),
                  pl.BlockSpec((1, N_MOD, D_MODEL), mod_map),
                  _const_spec((1, D_MODEL)),
                  _layer_spec((D_MODEL, FFN_DIM), layer),
                  _layer_spec((D_MODEL, FFN_DIM), layer),
                  _layer_spec((FFN_DIM, D_MODEL), layer)],
        out_specs=pl.BlockSpec((tm, D_MODEL), lambda i: (i, 0)),
        out_shape=jax.ShapeDtypeStruct((t, D_MODEL), F32),
        compiler_params=_params(1),
        name="ffn",
    )(x, mod, g, wg, wu, wd)


def _proj_kernel(x_ref, m_ref, g_ref, w_ref, ones_ref, gq_ref, gk_ref,
                 q_ref, k_ref, v_ref, pcz_ref):
    ones = ones_ref[...]

    def head_norm(t, gain):
        ss = _dot((t * t).astype(BF16), ones)
        return t * lax.rsqrt(ss * (1.0 / HEAD_DIM) + EPS) * gain

    a = ATTN_DIM
    o = 3 * a
    sub = x_ref.shape[0] // PROJ_SUBTILES
    for r in range(PROJ_SUBTILES):
        rows = slice(r * sub, (r + 1) * sub)
        n = _norm_mod(x_ref[rows, :], g_ref[...], m_ref[0, 3:4, :], m_ref[0, 4:5, :]).astype(BF16)
        proj = _dot(n, w_ref[:, :PROJ_COLS])
        q = head_norm(proj[:, 0:a], gq_ref[...])
        q_ref[rows, :] = (q * (HEAD_DIM ** -0.5)).astype(q_ref.dtype)
        k_ref[rows, :] = head_norm(proj[:, a:2 * a], gk_ref[...]).astype(k_ref.dtype)
        v_ref[rows, :] = proj[:, 2 * a:3 * a].astype(v_ref.dtype)
        u_pool = proj[:, o:o + POOL_DIM]
        u_conv = proj[:, o + POOL_DIM:o + POOL_DIM + CONV_DIM]
        gate_b = proj[:, o + POOL_DIM + CONV_DIM:o + POOL_DIM + 2 * CONV_DIM]
        gate_c = proj[:, o + POOL_DIM + 2 * CONV_DIM:o + POOL_DIM + 3 * CONV_DIM]
        pcz_ref[rows, 0:POOL_DIM] = u_pool
        pcz_ref[rows, POOL_DIM:POOL_DIM + CONV_DIM] = gate_c * u_conv
        pcz_ref[rows, POOL_DIM + CONV_DIM:PCZ_COLS] = gate_b


def _proj(x, mod, g, w, layer, ones, gq, gk, seq_len, kv_dtype):
    t = x.shape[0]
    tm = PROJ_TILE
    if mod.shape[0] == 1:
        mod_map = lambda i: (0, 0, 0)
    else:
        tiles_per_group = seq_len // tm
        mod_map = lambda i: (i // tiles_per_group, 0, 0)
    row_spec = lambda c: pl.BlockSpec((tm, c), lambda i: (i, 0))
    return pl.pallas_call(
        _proj_kernel,
        grid=(t // tm,),
        in_specs=[row_spec(D_MODEL),
                  pl.BlockSpec((1, N_MOD, D_MODEL), mod_map),
                  _const_spec((1, D_MODEL)),
                  _layer_spec((D_MODEL, PROJ_COLS + MERGE_COLS), layer),
                  _const_spec((ATTN_DIM, ATTN_DIM)),
                  _const_spec((1, ATTN_DIM)),
                  _const_spec((1, ATTN_DIM))],
        out_specs=[row_spec(ATTN_DIM), row_spec(ATTN_DIM), row_spec(ATTN_DIM), row_spec(PCZ_COLS)],
        out_shape=[jax.ShapeDtypeStruct((t, ATTN_DIM), BF16),
                   jax.ShapeDtypeStruct((t, ATTN_DIM), kv_dtype),
                   jax.ShapeDtypeStruct((t, ATTN_DIM), kv_dtype),
                   jax.ShapeDtypeStruct((t, PCZ_COLS), F32)],
        compiler_params=_params(1),
        name="proj",
    )(x, mod, g, w, ones, gq, gk)


def _first_head_lanes():
    return lax.broadcasted_iota(jnp.int32, (1, LANES), 1) < HEAD_DIM


def _ctx_attn_kernel(*refs, n_prev):
    q_ref, k_ref, v_ref = refs[:3]
    o_ref = refs[3 + 2 * n_prev]
    if n_prev:
        layers = [(refs[3 + 2 * d], refs[4 + 2 * d]) for d in range(n_prev)] + [(k_ref, v_ref)]
        for dst_ref, which in ((refs[-2], 0), (refs[-1], 1)):
            for d, kv in enumerate(layers):
                for t in range(ATTN_DIM // LANES):
                    tile = kv[which][:, t * LANES:(t + 1) * LANES]
                    dst_ref[0, d, 2 * t] = tile[:, :HEAD_DIM]
                    dst_ref[0, d, 2 * t + 1] = tile[:, HEAD_DIM:]
    first = _first_head_lanes()
    for t in range(ATTN_DIM // LANES):
        sl = slice(t * LANES, (t + 1) * LANES)
        qt = q_ref[:, sl]
        kt = k_ref[:, sl].astype(BF16)
        vt = v_ref[:, sl].astype(BF16)
        outs = []
        for half in range(2):
            keep = first if half == 0 else jnp.logical_not(first)
            s = _dot_nt(jnp.where(keep, qt, jnp.zeros_like(qt)), kt)
            e = jnp.exp(s - jnp.max(s, axis=-1, keepdims=True))
            den = jnp.sum(e, axis=-1, keepdims=True)
            outs.append(_dot(e.astype(BF16), vt) / den)
        o_ref[:, sl] = jnp.where(first, outs[0], outs[1]).astype(o_ref.dtype)


def _ctx_attention(q, k, v, seq_len, prev_kv=()):
    t = q.shape[0]
    n_seq = t // seq_len
    n_prev = len(prev_kv)
    spec = pl.BlockSpec((seq_len, ATTN_DIM), lambda i: (i, 0))
    out_specs = [spec]
    out_shape = [jax.ShapeDtypeStruct((t, ATTN_DIM), BF16)]
    if n_prev:
        heads_shape = (n_seq, n_prev + 1, N_HEADS, seq_len, HEAD_DIM)
        heads_spec = pl.BlockSpec((1,) + heads_shape[1:], lambda i: (i, 0, 0, 0, 0))
        out_specs += [heads_spec, heads_spec]
        out_shape += [jax.ShapeDtypeStruct(heads_shape, k.dtype)] * 2
    return pl.pallas_call(
        functools.partial(_ctx_attn_kernel, n_prev=n_prev),
        grid=(n_seq,),
        in_specs=[spec] * (3 + 2 * n_prev),
        out_specs=out_specs,
        out_shape=out_shape,
        compiler_params=_params(1),
        name="ctx_attention",
    )(q, k, v, *[z for kv in prev_kv for z in kv])


def _row_max_lanes(s):
    m = s[:, :LANES]
    for p in range(1, s.shape[1] // LANES):
        m = jnp.maximum(m, s[:, p * LANES:(p + 1) * LANES])
    return jnp.broadcast_to(jnp.max(m, axis=-1, keepdims=True), m.shape)


def _exp_shifted(s, m):
    return jnp.concatenate([jnp.exp(s[:, p * LANES:(p + 1) * LANES] - m)
                            for p in range(s.shape[1] // LANES)], axis=1)


def _na_kernel(q_ref, k_ref, v_ref, kc_ref, vc_ref, bias_ref, o_ref,
               qs_ref, oc_ref, mc_ref, lc_ref, *, n_rows):
    first = _first_head_lanes()
    win_keys = WIN_ROWS * GRID_W
    pair_rows = 2 * GRID_W
    n_pairs = ATTN_DIM // LANES
    j = pl.program_id(1)

    past = kc_ref.shape[1]
    ones_past = jnp.ones((past, LANES), BF16)
    for t in range(n_pairs):
        sl = slice(t * LANES, (t + 1) * LANES)
        for r in range(NA_ROWS):
            qt = q_ref[0, r * GRID_W:(r + 1) * GRID_W, sl]
            zero = jnp.zeros_like(qt)
            qs_ref[t, r * pair_rows:r * pair_rows + GRID_W, :] = jnp.where(first, qt, zero)
            qs_ref[t, r * pair_rows + GRID_W:(r + 1) * pair_rows, :] = jnp.where(first, zero, qt)
        kct = kc_ref[0, :, sl]
        vc_aug = jnp.concatenate([vc_ref[0, :, sl], ones_past], axis=1)
        for c in range(qs_ref.shape[1] // NA_CTX_CHUNK):
            rows = slice(c * NA_CTX_CHUNK, (c + 1) * NA_CTX_CHUNK)
            s = _dot_nt(qs_ref[t, rows, :], kct)
            m = _row_max_lanes(s)
            pv = _dot(_exp_shifted(s, m).astype(BF16), vc_aug)
            mc_ref[t, rows, :] = m
            oc_ref[t, rows, :] = pv[:, :LANES]
            lc_ref[t, rows, :] = pv[:, LANES:]

    ones_win = jnp.ones((win_keys, LANES), BF16)

    def row(r, carry):
        i = j * NA_ROWS + r
        rs = jnp.clip(i - WIN_ROWS // 2, 0, n_rows - WIN_ROWS)
        off0 = rs - i + (WIN_ROWS - 1)
        kstart = pl.multiple_of(rs * GRID_W, GRID_W)
        qstart = pl.multiple_of(r * GRID_W, GRID_W)
        rows = pl.ds(pl.multiple_of(r * pair_rows, pair_rows), pair_rows)
        for t in range(n_pairs):
            sl = slice(t * LANES, (t + 1) * LANES)
            kt = k_ref[0, pl.ds(kstart, win_keys), sl]
            v_aug = jnp.concatenate([v_ref[0, pl.ds(kstart, win_keys), sl], ones_win], axis=1)
            bias = jnp.concatenate([bias_ref[t, off0 + 2 * p] for p in range(WIN_ROWS // 2)], axis=1)
            s = _dot_nt(qs_ref[t, rows, :], kt) + bias
            m_ctx = mc_ref[t, rows, :]
            m = jnp.maximum(_row_max_lanes(s), m_ctx)
            w_ctx = jnp.exp(m_ctx - m)
            pv = _dot(_exp_shifted(s, m).astype(BF16), v_aug)
            o = ((pv[:, :LANES] + oc_ref[t, rows, :] * w_ctx)
                 / (pv[:, LANES:] + lc_ref[t, rows, :] * w_ctx))
            o_ref[0, pl.ds(qstart, GRID_W), sl] = jnp.where(first, o[:GRID_W], o[GRID_W:]).astype(o_ref.dtype)
        return carry

    lax.fori_loop(0, NA_ROWS, row, 0, unroll=True)


def _na_attention(q, k, v, kc, vc, bias):
    b, t, _ = q.shape
    past = kc.shape[1]
    n_rows = t // GRID_W
    n_pairs = ATTN_DIM // LANES
    stacked = 2 * NA_ROWS * GRID_W
    q_spec = pl.BlockSpec((1, NA_ROWS * GRID_W, ATTN_DIM), lambda bi, j: (bi, j, 0))
    img_spec = pl.BlockSpec((1, t, ATTN_DIM), lambda bi, j: (bi, 0, 0))
    past_spec = pl.BlockSpec((1, past, ATTN_DIM), lambda bi, j: (bi, 0, 0))
    return pl.pallas_call(
        functools.partial(_na_kernel, n_rows=n_rows),
        grid=(b, n_rows // NA_ROWS),
        in_specs=[q_spec, img_spec, img_spec, past_spec, past_spec, _const_spec(bias.shape)],
        out_specs=q_spec,
        out_shape=jax.ShapeDtypeStruct((b, t, ATTN_DIM), BF16),
        scratch_shapes=[pltpu.VMEM((n_pairs, stacked, LANES), BF16),
                        pltpu.VMEM((n_pairs, stacked, LANES), F32),
                        pltpu.VMEM((n_pairs, stacked, LANES), F32),
                        pltpu.VMEM((n_pairs, stacked, LANES), F32)],
        compiler_params=_params(2),
        name="na_attention",
    )(q, k, v, kc, vc, bias)


def _bias_kernel(r_ref, sel_ref, mask_ref, o_ref):
    r = r_ref[...]
    hi = r.astype(BF16)
    r1 = r - hi.astype(F32)
    mid = r1.astype(BF16)
    lo = (r1 - mid.astype(F32)).astype(BF16)
    sel = sel_ref[...]
    o_ref[...] = _dot(hi, sel) + _dot(mid, sel) + _dot(lo, sel) + mask_ref[...]


def _na_bias_tables(rpb):
    n_rel = 2 * WIN_COLS - 1
    qc = np.arange(GRID_W)[:, None]
    kc = np.arange(GRID_W)[None, :]
    start = np.clip(qc - WIN_COLS // 2, 0, GRID_W - WIN_COLS)
    inside = ((kc >= start) & (kc < start + WIN_COLS)).reshape(-1)
    rel = (kc - qc + WIN_COLS - 1).reshape(-1)
    sel = (np.arange(LANES)[:, None] == rel[None, :]) & inside[None, :]
    mask = np.where(inside, 0.0, MASK_VALUE)[None, :]
    n_off = 2 * WIN_ROWS - 1
    rows = DEPTH * N_HEADS * n_off
    table = jnp.pad(rpb.reshape(rows, n_rel), ((0, 0), (0, LANES - n_rel)))
    dense = pl.pallas_call(
        _bias_kernel,
        out_shape=jax.ShapeDtypeStruct((rows, GRID_W * GRID_W), F32),
        compiler_params=pltpu.CompilerParams(vmem_limit_bytes=VMEM_LIMIT),
        name="na_bias",
    )(table, jnp.asarray(sel, dtype=BF16), jnp.asarray(mask, dtype=F32))
    dense = dense.reshape(DEPTH, N_HEADS // 2, 2, n_off, GRID_W, GRID_W)
    two_rows = jnp.concatenate([dense[:, :, :, :-1], dense[:, :, :, 1:]], axis=-1)
    return two_rows.transpose(0, 1, 3, 2, 4, 5).reshape(
        DEPTH, N_HEADS // 2, n_off - 1, 2 * GRID_W, 2 * GRID_W)


def _merge_kernel(x_ref, m_ref, g_ref, wm_ref, a_ref, pcz_ref, prev_ref, next_ref,
                  wpool_ref, pscale_ref, wconv_ref, bconv_ref, wa_ref, wp_ref, wc_ref, wo_ref,
                  o_ref, buf_ref, *, seq_len):
    tm = x_ref.shape[1]
    j = pl.program_id(1)
    n_tiles = pl.num_programs(1)

    buf_ref[HALO:HALO + tm, :] = pcz_ref[0]
    buf_ref[0:HALO, :] = jnp.where(j > 0, prev_ref[0], 0.0)
    buf_ref[HALO + tm:HALO + tm + HALO, :] = jnp.where(j < n_tiles - 1, next_ref[0], 0.0)

    lane = lax.broadcasted_iota(jnp.int32, (1, LANES), 1)
    upper = lane >= POOL_GROUP_DIM
    sub = MERGE_SUB_ROWS
    for r in range(tm // sub):
        base = r * sub

        def shifted(d, cols):
            return buf_ref[HALO + base + d:HALO + base + d + sub, cols]

        pos = j * tm + base + lax.broadcasted_iota(jnp.int32, (sub, 1), 0)
        diffs = []
        for p in range(POOL_DIM // LANES):
            cols = slice(p * LANES, (p + 1) * LANES)
            w_lo, w_hi = POOL_WINDOWS[2 * p], POOL_WINDOWS[2 * p + 1]
            centre = shifted(0, cols)
            common = centre
            for d in range(-(w_lo // 2), w_lo // 2):
                if d != 0:
                    common = common + shifted(d, cols)
            extra = None
            for d in list(range(-(w_hi // 2), -(w_lo // 2))) + list(range(w_lo // 2, w_hi // 2)):
                extra = shifted(d, cols) if extra is None else extra + shifted(d, cols)
            total = common + jnp.where(upper, extra, 0.0)
            half = jnp.where(upper, w_hi // 2, w_lo // 2)
            count = jnp.minimum(pos + half, seq_len) - jnp.maximum(pos - half, 0)
            diffs.append(total / count.astype(F32) - centre)
        diff = jnp.concatenate(diffs, axis=1).astype(BF16)
        pooled = _dot(diff, wpool_ref[...]) * pscale_ref[...]

        zc = slice(POOL_DIM, POOL_DIM + CONV_DIM)
        conv = (wconv_ref[0:1, :] * shifted(-1, zc) + wconv_ref[1:2, :] * shifted(0, zc)
                + wconv_ref[2:3, :] * shifted(1, zc) + bconv_ref[...])
        convd = shifted(0, slice(POOL_DIM + CONV_DIM, PCZ_COLS)) * conv

        rows = slice(base, base + sub)
        x = x_ref[0, rows, :]
        n = _norm_mod(x, g_ref[...], m_ref[0, 3:4, :], m_ref[0, 4:5, :]).astype(BF16)
        merged = None
        branches = ((a_ref[0, rows, :], wa_ref), (pooled.astype(BF16), wp_ref),
                    (convd.astype(BF16), wc_ref))
        for i, (branch_in, w_ref) in enumerate(branches):
            lo = PROJ_COLS + i * D_MODEL
            gate = _sigmoid(_dot(n, wm_ref[:, lo:lo + D_MODEL]))
            term = gate * _dot(branch_in, w_ref[...])
            merged = term if merged is None else merged + term
        o_ref[0, rows, :] = x + m_ref[0, 5:6, :] * _dot(merged.astype(BF16), wo_ref[...])


def _merge(x, mod, g, wm, layer, a, pcz, wpool, pscale, wconv, bconv, wa, wp, wc, wo, tm):
    b, seq_len, _ = x.shape
    n_tiles = seq_len // tm
    halo_per_tile = tm // HALO
    n_halo = seq_len // HALO
    mod_map = (lambda bi, j: (0, 0, 0)) if mod.shape[0] == 1 else (lambda bi, j: (bi, 0, 0))
    tile = lambda c: pl.BlockSpec((1, tm, c), lambda bi, j: (bi, j, 0))
    prev_spec = pl.BlockSpec((1, HALO, PCZ_COLS),
                             lambda bi, j: (bi, jnp.maximum(j * halo_per_tile - 1, 0), 0))
    next_spec = pl.BlockSpec((1, HALO, PCZ_COLS),
                             lambda bi, j: (bi, jnp.minimum((j + 1) * halo_per_tile, n_halo - 1), 0))
    return pl.pallas_call(
        functools.partial(_merge_kernel, seq_len=seq_len),
        grid=(b, n_tiles),
        in_specs=[tile(D_MODEL),
                  pl.BlockSpec((1, N_MOD, D_MODEL), mod_map),
                  _const_spec((1, D_MODEL)),
                  _layer_spec((D_MODEL, PROJ_COLS + MERGE_COLS), layer),
                  tile(ATTN_DIM), tile(PCZ_COLS), prev_spec, next_spec,
                  _layer_spec((POOL_DIM, POOL_DIM), layer),
                  _const_spec((1, POOL_DIM)),
                  _const_spec((3, CONV_DIM)),
                  _const_spec((1, CONV_DIM)),
                  _layer_spec((ATTN_DIM, D_MODEL), layer),
                  _layer_spec((POOL_DIM, D_MODEL), layer),
                  _layer_spec((CONV_DIM, D_MODEL), layer),
                  _layer_spec((D_MODEL, D_MODEL), layer)],
        out_specs=tile(D_MODEL),
        out_shape=jax.ShapeDtypeStruct((b, seq_len, D_MODEL), F32),
        scratch_shapes=[pltpu.VMEM((tm + 2 * HALO, PCZ_COLS), F32)],
        compiler_params=_params(2),
        name="merge",
    )(x, mod, g, wm, a, pcz, pcz, pcz, wpool, pscale, wconv, bconv, wa, wp, wc, wo)


def _head_ones():
    blk = np.arange(ATTN_DIM) // HEAD_DIM
    return jnp.asarray(blk[:, None] == blk[None, :], dtype=BF16)


def _layer(x, mod, sw, lw, layer, attend, merge_tile, kv_dtype):
    b, seq_len, _ = x.shape
    t = b * seq_len
    x2 = x.reshape(t, D_MODEL)
    x2 = _ffn(x2, mod, lw['g_ffn1'], sw['w1g'], sw['w1u'], sw['w1d'], layer, 0, seq_len)
    q, k, v, pcz = _proj(x2, mod, lw['g_mix'], sw['w_in'], layer, sw['ones'], lw['g_q'], lw['g_k'],
                         seq_len, kv_dtype)
    a, extras = attend(q, k, v)
    x3 = _merge(x2.reshape(b, seq_len, D_MODEL), mod, lw['g_mix'], sw['w_in'], layer,
                a.reshape(b, seq_len, ATTN_DIM), pcz.reshape(b, seq_len, PCZ_COLS),
                sw['w_pool'], lw['pool_scale'], lw['w_conv'], lw['b_conv'],
                sw['w_br_attn'], sw['w_br_pool'], sw['w_br_conv'], sw['w_out'], merge_tile)
    x2 = _ffn(x3.reshape(t, D_MODEL), mod, lw['g_ffn2'], sw['w2g'], sw['w2u'], sw['w2d'], layer, 6,
              seq_len)
    return x2.reshape(b, seq_len, D_MODEL), k, v, extras


def kernel(x_prompt, x_sample, cache_k, cache_v, c, c_ctx, w_mod, b_mod, g_ffn1, w_ffn1_gate, w_ffn1_up, w_ffn1_down, g_mix, w_in, g_q, g_k, rpb, w_pool, pool_scale, w_conv, b_conv, w_br_attn, w_br_pool, w_br_conv, w_out, g_ffn2, w_ffn2_gate, w_ffn2_up, w_ffn2_down):
    batch, seq, _ = x_prompt.shape
    dec_batch, dec_seq, _ = x_sample.shape
    past = cache_k.shape[3]
    assert dec_batch + 1 <= SUBLANES

    cvec = jnp.zeros((SUBLANES, D_MODEL), F32).at[0].set(c_ctx).at[1:1 + dec_batch].set(c)
    mod = _mod_table(cvec, w_mod, b_mod).reshape(DEPTH, SUBLANES, N_MOD, D_MODEL)

    na_bias = _na_bias_tables(rpb)
    group_eye = jnp.eye(POOL_GROUPS, dtype=F32)
    sw = {
        'w1g': w_ffn1_gate.astype(BF16), 'w1u': w_ffn1_up.astype(BF16), 'w1d': w_ffn1_down.astype(BF16),
        'w2g': w_ffn2_gate.astype(BF16), 'w2u': w_ffn2_up.astype(BF16), 'w2d': w_ffn2_down.astype(BF16),
        'w_in': w_in.astype(BF16), 'ones': _head_ones(),
        'w_pool': (group_eye[None, :, None, :, None] * w_pool[:, :, :, None, :]
                   ).reshape(DEPTH, POOL_DIM, POOL_DIM).astype(BF16),
        'w_br_attn': w_br_attn.astype(BF16), 'w_br_pool': w_br_pool.astype(BF16),
        'w_br_conv': w_br_conv.astype(BF16), 'w_out': w_out.astype(BF16),
    }
    from_heads = lambda z: z.transpose(0, 2, 1, 3).reshape(dec_batch, past, ATTN_DIM).astype(BF16)
    xp, xs = x_prompt, x_sample
    ctx_kv = []
    heads = None
    for l in range(DEPTH):
        row = lambda p: p[l].reshape(1, -1)
        lw = {
            'g_ffn1': row(g_ffn1), 'g_mix': row(g_mix), 'g_ffn2': row(g_ffn2),
            'g_q': jnp.tile(g_q[l], N_HEADS).reshape(1, ATTN_DIM),
            'g_k': jnp.tile(g_k[l], N_HEADS).reshape(1, ATTN_DIM),
            'pool_scale': row(pool_scale), 'w_conv': w_conv[l], 'b_conv': row(b_conv),
        }
        emit_heads = l == DEPTH - 1

        def attend_ctx(q, k, v):
            res = _ctx_attention(q, k, v, seq, prev_kv=tuple(ctx_kv) if emit_heads else ())
            return res[0], tuple(res[1:])

        xp, kp, vp, extras = _layer(xp, mod[l, 0:1], sw, lw, l, attend_ctx, seq, F32)
        ctx_kv.append((kp, vp))
        if emit_heads:
            heads = extras
        kc, vc = from_heads(cache_k[:, l]), from_heads(cache_v[:, l])
        bias = na_bias[l]

        def attend_lat(q, k, v):
            r3 = lambda z: z.reshape(dec_batch, dec_seq, ATTN_DIM)
            a = _na_attention(r3(q), r3(k), r3(v), kc, vc, bias)
            return a.reshape(dec_batch * dec_seq, ATTN_DIM), ()

        xs, _, _, _ = _layer(xs, mod[l, 1:1 + dec_batch], sw, lw, l, attend_lat, MERGE_TILE_LATENT, BF16)
    new_k, new_v = heads
    return (xp, xs, new_k, new_v)
```

```python
import functools

import numpy as np
import jax
import jax.numpy as jnp
from jax import lax
from jax.experimental import pallas as pl
from jax.experimental.pallas import tpu as pltpu

D_MODEL = 1024
DEPTH = 2
GRID_W = 64
WIN_ROWS = 8
WIN_COLS = 16
N_HEADS = 8
HEAD_DIM = 64
ATTN_DIM = N_HEADS * HEAD_DIM
POOL_GROUPS = 4
POOL_GROUP_DIM = 64
POOL_DIM = POOL_GROUPS * POOL_GROUP_DIM
POOL_WINDOWS = (2, 4, 8, 16)
CONV_DIM = 256
FFN_DIM = 2816
N_MOD = 9
EPS = 1e-6
PROJ_COLS = 3 * ATTN_DIM + POOL_DIM + 3 * CONV_DIM
MERGE_COLS = 3 * D_MODEL
PCZ_COLS = POOL_DIM + 2 * CONV_DIM

LANES = 128
SUBLANES = 8
HALO = SUBLANES
MASK_VALUE = -1e30
VMEM_LIMIT = 56 * 1024 * 1024

FFN_TILE = 1024
FFN_SUBTILES = 4
MXU_DIM = 256
FFN_SPLITS = (0, (FFN_DIM // MXU_DIM // 2) * MXU_DIM, FFN_DIM)
PROJ_TILE = 512
PROJ_SUBTILES = 1
MERGE_SUB_ROWS = 256
MERGE_TILE_LATENT = 1024
NA_ROWS = 8
NA_CTX_CHUNK = 256

F32 = jnp.float32
BF16 = jnp.bfloat16


def _dot(a, b):
    return jnp.dot(a, b, preferred_element_type=F32)


def _dot_nt(a, b):
    return lax.dot_general(a, b, (((1,), (1,)), ((), ())), preferred_element_type=F32)


def _split_bf16(x):
    hi = x.astype(BF16)
    lo = (x - hi.astype(F32)).astype(BF16)
    return hi, lo


def _sigmoid(x):
    return 0.5 * jnp.tanh(0.5 * x) + 0.5


def _norm_mod(x, g, shift, scale):
    ms = jnp.mean(x * x, axis=-1, keepdims=True)
    y = x * lax.rsqrt(ms + EPS) * g
    return y * (1.0 + scale) + shift


def _const_spec(shape):
    nd = len(shape)
    return pl.BlockSpec(shape, lambda *_: (0,) * nd, pipeline_mode=pl.Buffered(1))


def _layer_spec(shape, layer):
    nd = len(shape)
    return pl.BlockSpec((None,) + tuple(shape), lambda *_: (layer,) + (0,) * nd,
                        pipeline_mode=pl.Buffered(1))


def _params(n_axes):
    return pltpu.CompilerParams(dimension_semantics=("arbitrary",) * n_axes,
                                vmem_limit_bytes=VMEM_LIMIT)


def _mod_kernel(c_ref, w_ref, b_ref, o_ref):
    c = c_ref[...]
    a = c * jax.nn.sigmoid(c)
    a_hi, a_lo = _split_bf16(a)
    w_hi, w_lo = _split_bf16(w_ref[0])
    acc = _dot(a_hi, w_hi) + _dot(a_lo, w_hi) + _dot(a_hi, w_lo)
    o_ref[0] = acc + b_ref[0]


def _mod_table(cvec, w_mod, b_mod):
    nb = D_MODEL
    cols = N_MOD * D_MODEL
    return pl.pallas_call(
        _mod_kernel,
        grid=(DEPTH, cols // nb),
        in_specs=[pl.BlockSpec((SUBLANES, D_MODEL), lambda l, n: (0, 0)),
                  pl.BlockSpec((1, D_MODEL, nb), lambda l, n: (l, 0, n)),
                  pl.BlockSpec((1, 1, nb), lambda l, n: (l, 0, n))],
        out_specs=pl.BlockSpec((1, SUBLANES, nb), lambda l, n: (l, 0, n)),
        out_shape=jax.ShapeDtypeStruct((DEPTH, SUBLANES, cols), F32),
        compiler_params=_params(2),
        name="adaln_table",
    )(cvec, w_mod, b_mod.reshape(DEPTH, 1, cols))


def _ffn_kernel(x_ref, m_ref, g_ref, wg_ref, wu_ref, wd_ref, o_ref, *, mi):
    shift = m_ref[0, mi:mi + 1, :]
    scale = m_ref[0, mi + 1:mi + 2, :]
    gate = m_ref[0, mi + 2:mi + 3, :]
    sub = x_ref.shape[0] // FFN_SUBTILES
    for r in range(FFN_SUBTILES):
        rows = slice(r * sub, (r + 1) * sub)
        x = x_ref[rows, :]
        h = _norm_mod(x, g_ref[...], shift, scale).astype(BF16)
        acc = None
        for lo, hi in zip(FFN_SPLITS[:-1], FFN_SPLITS[1:]):
            sl = slice(lo, hi)
            a = _dot(h, wg_ref[:, sl])
            b = _dot(h, wu_ref[:, sl])
            s = (a * _sigmoid(a) * b).astype(BF16)
            y = _dot(s, wd_ref[sl, :])
            acc = y if acc is None else acc + y
        o_ref[rows, :] = x + (0.5 * gate) * acc


def _ffn(x, mod, g, wg, wu, wd, layer, mi, seq_len):
    t = x.shape[0]
    tm = FFN_TILE
    if mod.shape[0] == 1:
        mod_map = lambda i: (0, 0, 0)
    else:
        tiles_per_group = seq_len // tm
        mod_map = lambda i: (i // tiles_per_group, 0, 0)
    return pl.pallas_call(
        functools.partial(_ffn_kernel, mi=mi),
        grid=(t // tm,),
        in_specs=[pl.BlockSpec((tm, D_MODEL), lambda i: (i, 0)),
                  pl.BlockSpec((1, N_MOD, D_MODEL), mod_map),
                  _const_spec((1, D_MODEL)),
                  _layer_spec((D_MODEL, FFN_DIM), layer),
                  _layer_spec((D_MODEL, FFN_DIM), layer),
                  _layer_spec((FFN_DIM, D_MODEL), layer)],
        out_specs=pl.BlockSpec((tm, D_MODEL), lambda i: (i, 0)),
        out_shape=jax.ShapeDtypeStruct((t, D_MODEL), F32),
        compiler_params=_params(1),
        name="ffn",
    )(x, mod, g, wg, wu, wd)


def _proj_kernel(x_ref, m_ref, g_ref, w_ref, ones_ref, gq_ref, gk_ref,
                 q_ref, k_ref, v_ref, pcz_ref):
    ones = ones_ref[...]

    def head_norm(t, gain):
        ss = _dot((t * t).astype(BF16), ones)
        return t * lax.rsqrt(ss * (1.0 / HEAD_DIM) + EPS) * gain

    a = ATTN_DIM
    o = 3 * a
    sub = x_ref.shape[0] // PROJ_SUBTILES
    for r in range(PROJ_SUBTILES):
        rows = slice(r * sub, (r + 1) * sub)
        n = _norm_mod(x_ref[rows, :], g_ref[...], m_ref[0, 3:4, :], m_ref[0, 4:5, :]).astype(BF16)
        proj = _dot(n, w_ref[:, :PROJ_COLS])
        q = head_norm(proj[:, 0:a], gq_ref[...])
        q_ref[rows, :] = (q * (HEAD_DIM ** -0.5)).astype(q_ref.dtype)
        k_ref[rows, :] = head_norm(proj[:, a:2 * a], gk_ref[...]).astype(k_ref.dtype)
        v_ref[rows, :] = proj[:, 2 * a:3 * a].astype(v_ref.dtype)
        u_pool = proj[:, o:o + POOL_DIM]
        u_conv = proj[:, o + POOL_DIM:o + POOL_DIM + CONV_DIM]
        gate_b = proj[:, o + POOL_DIM + CONV_DIM:o + POOL_DIM + 2 * CONV_DIM]
        gate_c = proj[:, o + POOL_DIM + 2 * CONV_DIM:o + POOL_DIM + 3 * CONV_DIM]
        pcz_ref[rows, 0:POOL_DIM] = u_pool
        pcz_ref[rows, POOL_DIM:POOL_DIM + CONV_DIM] = gate_c * u_conv
        pcz_ref[rows, POOL_DIM + CONV_DIM:PCZ_COLS] = gate_b


def _proj(x, mod, g, w, layer, ones, gq, gk, seq_len, kv_dtype):
    t = x.shape[0]
    tm = PROJ_TILE
    if mod.shape[0] == 1:
        mod_map = lambda i: (0, 0, 0)
    else:
        tiles_per_group = seq_len // tm
        mod_map = lambda i: (i // tiles_per_group, 0, 0)
    row_spec = lambda c: pl.BlockSpec((tm, c), lambda i: (i, 0))
    return pl.pallas_call(
        _proj_kernel,
        grid=(t // tm,),
        in_specs=[row_spec(D_MODEL),
                  pl.BlockSpec((1, N_MOD, D_MODEL), mod_map),
                  _const_spec((1, D_MODEL)),
                  _layer_spec((D_MODEL, PROJ_COLS + MERGE_COLS), layer),
                  _const_spec((ATTN_DIM, ATTN_DIM)),
                  _const_spec((1, ATTN_DIM)),
                  _const_spec((1, ATTN_DIM))],
        out_specs=[row_spec(ATTN_DIM), row_spec(ATTN_DIM), row_spec(ATTN_DIM), row_spec(PCZ_COLS)],
        out_shape=[jax.ShapeDtypeStruct((t, ATTN_DIM), BF16),
                   jax.ShapeDtypeStruct((t, ATTN_DIM), kv_dtype),
                   jax.ShapeDtypeStruct((t, ATTN_DIM), kv_dtype),
                   jax.ShapeDtypeStruct((t, PCZ_COLS), F32)],
        compiler_params=_params(1),
        name="proj",
    )(x, mod, g, w, ones, gq, gk)


def _first_head_lanes():
    return lax.broadcasted_iota(jnp.int32, (1, LANES), 1) < HEAD_DIM


def _ctx_attn_kernel(*refs, n_prev):
    q_ref, k_ref, v_ref = refs[:3]
    o_ref = refs[3 + 2 * n_prev]
    if n_prev:
        layers = [(refs[3 + 2 * d], refs[4 + 2 * d]) for d in range(n_prev)] + [(k_ref, v_ref)]
        for dst_ref, which in ((refs[-2], 0), (refs[-1], 1)):
            for d, kv in enumerate(layers):
                for t in range(ATTN_DIM // LANES):
                    tile = kv[which][:, t * LANES:(t + 1) * LANES]
                    dst_ref[0, d, 2 * t] = tile[:, :HEAD_DIM]
                    dst_ref[0, d, 2 * t + 1] = tile[:, HEAD_DIM:]
    first = _first_head_lanes()
    for t in range(ATTN_DIM // LANES):
        sl = slice(t * LANES, (t + 1) * LANES)
        qt = q_ref[:, sl]
        kt = k_ref[:, sl].astype(BF16)
        vt = v_ref[:, sl].astype(BF16)
        outs = []
        for half in range(2):
            keep = first if half == 0 else jnp.logical_not(first)
            s = _dot_nt(jnp.where(keep, qt, jnp.zeros_like(qt)), kt)
            e = jnp.exp(s - jnp.max(s, axis=-1, keepdims=True))
            den = jnp.sum(e, axis=-1, keepdims=True)
            outs.append(_dot(e.astype(BF16), vt) / den)
        o_ref[:, sl] = jnp.where(first, outs[0], outs[1]).astype(o_ref.dtype)


def _ctx_attention(q, k, v, seq_len, prev_kv=()):
    t = q.shape[0]
    n_seq = t // seq_len
    n_prev = len(prev_kv)
    spec = pl.BlockSpec((seq_len, ATTN_DIM), lambda i: (i, 0))
    out_specs = [spec]
    out_shape = [jax.ShapeDtypeStruct((t, ATTN_DIM), BF16)]
    if n_prev:
        heads_shape = (n_seq, n_prev + 1, N_HEADS, seq_len, HEAD_DIM)
        heads_spec = pl.BlockSpec((1,) + heads_shape[1:], lambda i: (i, 0, 0, 0, 0))
        out_specs += [heads_spec, heads_spec]
        out_shape += [jax.ShapeDtypeStruct(heads_shape, k.dtype)] * 2
    return pl.pallas_call(
        functools.partial(_ctx_attn_kernel, n_prev=n_prev),
        grid=(n_seq,),
        in_specs=[spec] * (3 + 2 * n_prev),
        out_specs=out_specs,
        out_shape=out_shape,
        compiler_params=_params(1),
        name="ctx_attention",
    )(q, k, v, *[z for kv in prev_kv for z in kv])


def _row_max_lanes(s):
    m = s[:, :LANES]
    for p in range(1, s.shape[1] // LANES):
        m = jnp.maximum(m, s[:, p * LANES:(p + 1) * LANES])
    return jnp.broadcast_to(jnp.max(m, axis=-1, keepdims=True), m.shape)


def _exp_shifted(s, m):
    return jnp.concatenate([jnp.exp(s[:, p * LANES:(p + 1) * LANES] - m)
                            for p in range(s.shape[1] // LANES)], axis=1)


def _na_kernel(q_ref, k_ref, v_ref, kc_ref, vc_ref, bias_ref, o_ref,
               qs_ref, oc_ref, mc_ref, lc_ref, *, n_rows):
    first = _first_head_lanes()
    win_keys = WIN_ROWS * GRID_W
    pair_rows = 2 * GRID_W
    n_pairs = ATTN_DIM // LANES
    j = pl.program_id(1)

    past = kc_ref.shape[1]
    ones_past = jnp.ones((past, LANES), BF16)
    for t in range(n_pairs):
        sl = slice(t * LANES, (t + 1) * LANES)
        for r in range(NA_ROWS):
            qt = q_ref[0, r * GRID_W:(r + 1) * GRID_W, sl]
            zero = jnp.zeros_like(qt)
            qs_ref[t, r * pair_rows:r * pair_rows + GRID_W, :] = jnp.where(first, qt, zero)
            qs_ref[t, r * pair_rows + GRID_W:(r + 1) * pair_rows, :] = jnp.where(first, zero, qt)
        kct = kc_ref[0, :, sl]
        vc_aug = jnp.concatenate([vc_ref[0, :, sl], ones_past], axis=1)
        for c in range(qs_ref.shape[1] // NA_CTX_CHUNK):
            rows = slice(c * NA_CTX_CHUNK, (c + 1) * NA_CTX_CHUNK)
            s = _dot_nt(qs_ref[t, rows, :], kct)
            m = _row_max_lanes(s)
            pv = _dot(_exp_shifted(s, m).astype(BF16), vc_aug)
            mc_ref[t, rows, :] = m
            oc_ref[t, rows, :] = pv[:, :LANES]
            lc_ref[t, rows, :] = pv[:, LANES:]

    ones_win = jnp.ones((win_keys, LANES), BF16)

    def row(r, carry):
        i = j * NA_ROWS + r
        rs = jnp.clip(i - WIN_ROWS // 2, 0, n_rows - WIN_ROWS)
        off0 = rs - i + (WIN_ROWS - 1)
        kstart = pl.multiple_of(rs * GRID_W, GRID_W)
        qstart = pl.multiple_of(r * GRID_W, GRID_W)
        rows = pl.ds(pl.multiple_of(r * pair_rows, pair_rows), pair_rows)
        for t in range(n_pairs):
            sl = slice(t * LANES, (t + 1) * LANES)
            kt = k_ref[0, pl.ds(kstart, win_keys), sl]
            v_aug = jnp.concatenate([v_ref[0, pl.ds(kstart, win_keys), sl], ones_win], axis=1)
            bias = jnp.concatenate([bias_ref[t, off0 + 2 * p] for p in range(WIN_ROWS // 2)], axis=1)
            s = _dot_nt(qs_ref[t, rows, :], kt) + bias
            m_ctx = mc_ref[t, rows, :]
            m = jnp.maximum(_row_max_lanes(s), m_ctx)
            w_ctx = jnp.exp(m_ctx - m)
            pv = _dot(_exp_shifted(s, m).astype(BF16), v_aug)
            o = ((pv[:, :LANES] + oc_ref[t, rows, :] * w_ctx)
                 / (pv[:, LANES:] + lc_ref[t, rows, :] * w_ctx))
            o_ref[0, pl.ds(qstart, GRID_W), sl] = jnp.where(first, o[:GRID_W], o[GRID_W:]).astype(o_ref.dtype)
        return carry

    lax.fori_loop(0, NA_ROWS, row, 0, unroll=True)


def _na_attention(q, k, v, kc, vc, bias):
    b, t, _ = q.shape
    past = kc.shape[1]
    n_rows = t // GRID_W
    n_pairs = ATTN_DIM // LANES
    stacked = 2 * NA_ROWS * GRID_W
    q_spec = pl.BlockSpec((1, NA_ROWS * GRID_W, ATTN_DIM), lambda bi, j: (bi, j, 0))
    img_spec = pl.BlockSpec((1, t, ATTN_DIM), lambda bi, j: (bi, 0, 0))
    past_spec = pl.BlockSpec((1, past, ATTN_DIM), lambda bi, j: (bi, 0, 0))
    return pl.pallas_call(
        functools.partial(_na_kernel, n_rows=n_rows),
        grid=(b, n_rows // NA_ROWS),
        in_specs=[q_spec, img_spec, img_spec, past_spec, past_spec, _const_spec(bias.shape)],
        out_specs=q_spec,
        out_shape=jax.ShapeDtypeStruct((b, t, ATTN_DIM), BF16),
        scratch_shapes=[pltpu.VMEM((n_pairs, stacked, LANES), BF16),
                        pltpu.VMEM((n_pairs, stacked, LANES), F32),
                        pltpu.VMEM((n_pairs, stacked, LANES), F32),
                        pltpu.VMEM((n_pairs, stacked, LANES), F32)],
        compiler_params=_params(2),
        name="na_attention",
    )(q, k, v, kc, vc, bias)


def _bias_kernel(r_ref, sel_ref, mask_ref, o_ref):
    r = r_ref[...]
    hi = r.astype(BF16)
    r1 = r - hi.astype(F32)
    mid = r1.astype(BF16)
    lo = (r1 - mid.astype(F32)).astype(BF16)
    sel = sel_ref[...]
    o_ref[...] = _dot(hi, sel) + _dot(mid, sel) + _dot(lo, sel) + mask_ref[...]


def _na_bias_tables(rpb):
    n_rel = 2 * WIN_COLS - 1
    qc = np.arange(GRID_W)[:, None]
    kc = np.arange(GRID_W)[None, :]
    start = np.clip(qc - WIN_COLS // 2, 0, GRID_W - WIN_COLS)
    inside = ((kc >= start) & (kc < start + WIN_COLS)).reshape(-1)
    rel = (kc - qc + WIN_COLS - 1).reshape(-1)
    sel = (np.arange(LANES)[:, None] == rel[None, :]) & inside[None, :]
    mask = np.where(inside, 0.0, MASK_VALUE)[None, :]
    n_off = 2 * WIN_ROWS - 1
    rows = DEPTH * N_HEADS * n_off
    table = jnp.pad(rpb.reshape(rows, n_rel), ((0, 0), (0, LANES - n_rel)))
    dense = pl.pallas_call(
        _bias_kernel,
        out_shape=jax.ShapeDtypeStruct((rows, GRID_W * GRID_W), F32),
        compiler_params=pltpu.CompilerParams(vmem_limit_bytes=VMEM_LIMIT),
        name="na_bias",
    )(table, jnp.asarray(sel, dtype=BF16), jnp.asarray(mask, dtype=F32))
    dense = dense.reshape(DEPTH, N_HEADS // 2, 2, n_off, GRID_W, GRID_W)
    two_rows = jnp.concatenate([dense[:, :, :, :-1], dense[:, :, :, 1:]], axis=-1)
    return two_rows.transpose(0, 1, 3, 2, 4, 5).reshape(
        DEPTH, N_HEADS // 2, n_off - 1, 2 * GRID_W, 2 * GRID_W)


def _merge_kernel(x_ref, m_ref, g_ref, wm_ref, a_ref, pcz_ref, prev_ref, next_ref,
                  wpool_ref, pscale_ref, wconv_ref, bconv_ref, wa_ref, wp_ref, wc_ref, wo_ref,
                  o_ref, buf_ref, *, seq_len):
    tm = x_ref.shape[1]
    j = pl.program_id(1)
    n_tiles = pl.num_programs(1)

    buf_ref[HALO:HALO + tm, :] = pcz_ref[0]
    buf_ref[0:HALO, :] = jnp.where(j > 0, prev_ref[0], 0.0)
    buf_ref[HALO + tm:HALO + tm + HALO, :] = jnp.where(j < n_tiles - 1, next_ref[0], 0.0)

    lane = lax.broadcasted_iota(jnp.int32, (1, LANES), 1)
    upper = lane >= POOL_GROUP_DIM
    sub = MERGE_SUB_ROWS
    for r in range(tm // sub):
        base = r * sub

        def shifted(d, cols):
            return buf_ref[HALO + base + d:HALO + base + d + sub, cols]

        pos = j * tm + base + lax.broadcasted_iota(jnp.int32, (sub, 1), 0)
        diffs = []
        for p in range(POOL_DIM // LANES):
            cols = slice(p * LANES, (p + 1) * LANES)
            w_lo, w_hi = POOL_WINDOWS[2 * p], POOL_WINDOWS[2 * p + 1]
            centre = shifted(0, cols)
            common = centre
            for d in range(-(w_lo // 2), w_lo // 2):
                if d != 0:
                    common = common + shifted(d, cols)
            extra = None
            for d in list(range(-(w_hi // 2), -(w_lo // 2))) + list(range(w_lo // 2, w_hi // 2)):
                extra = shifted(d, cols) if extra is None else extra + shifted(d, cols)
            total = common + jnp.where(upper, extra, 0.0)
            half = jnp.where(upper, w_hi // 2, w_lo // 2)
            count = jnp.minimum(pos + half, seq_len) - jnp.maximum(pos - half, 0)
            diffs.append(total / count.astype(F32) - centre)
        diff = jnp.concatenate(diffs, axis=1).astype(BF16)
        pooled = _dot(diff, wpool_ref[...]) * pscale_ref[...]

        zc = slice(POOL_DIM, POOL_DIM + CONV_DIM)
        conv = (wconv_ref[0:1, :] * shifted(-1, zc) + wconv_ref[1:2, :] * shifted(0, zc)
                + wconv_ref[2:3, :] * shifted(1, zc) + bconv_ref[...])
        convd = shifted(0, slice(POOL_DIM + CONV_DIM, PCZ_COLS)) * conv

        rows = slice(base, base + sub)
        x = x_ref[0, rows, :]
        n = _norm_mod(x, g_ref[...], m_ref[0, 3:4, :], m_ref[0, 4:5, :]).astype(BF16)
        merged = None
        branches = ((a_ref[0, rows, :], wa_ref), (pooled.astype(BF16), wp_ref),
                    (convd.astype(BF16), wc_ref))
        for i, (branch_in, w_ref) in enumerate(branches):
            lo = PROJ_COLS + i * D_MODEL
            gate = _sigmoid(_dot(n, wm_ref[:, lo:lo + D_MODEL]))
            term = gate * _dot(branch_in, w_ref[...])
            merged = term if merged is None else merged + term
        o_ref[0, rows, :] = x + m_ref[0, 5:6, :] * _dot(merged.astype(BF16), wo_ref[...])


def _merge(x, mod, g, wm, layer, a, pcz, wpool, pscale, wconv, bconv, wa, wp, wc, wo, tm):
    b, seq_len, _ = x.shape
    n_tiles = seq_len // tm
    halo_per_tile = tm // HALO
    n_halo = seq_len // HALO
    mod_map = (lambda bi, j: (0, 0, 0)) if mod.shape[0] == 1 else (lambda bi, j: (bi, 0, 0))
    tile = lambda c: pl.BlockSpec((1, tm, c), lambda bi, j: (bi, j, 0))
    prev_spec = pl.BlockSpec((1, HALO, PCZ_COLS),
                             lambda bi, j: (bi, jnp.maximum(j * halo_per_tile - 1, 0), 0))
    next_spec = pl.BlockSpec((1, HALO, PCZ_COLS),
                             lambda bi, j: (bi, jnp.minimum((j + 1) * halo_per_tile, n_halo - 1), 0))
    return pl.pallas_call(
        functools.partial(_merge_kernel, seq_len=seq_len),
        grid=(b, n_tiles),
        in_specs=[tile(D_MODEL),
                  pl.BlockSpec((1, N_MOD, D_MODEL), mod_map),
                  _const_spec((1, D_MODEL)),
                  _layer_spec((D_MODEL, PROJ_COLS + MERGE_COLS), layer),
                  tile(ATTN_DIM), tile(PCZ_COLS), prev_spec, next_spec,
                  _layer_spec((POOL_DIM, POOL_DIM), layer),
                  _const_spec((1, POOL_DIM)),
                  _const_spec((3, CONV_DIM)),
                  _const_spec((1, CONV_DIM)),
                  _layer_spec((ATTN_DIM, D_MODEL), layer),
                  _layer_spec((POOL_DIM, D_MODEL), layer),
                  _layer_spec((CONV_DIM, D_MODEL), layer),
                  _layer_spec((D_MODEL, D_MODEL), layer)],
        out_specs=tile(D_MODEL),
        out_shape=jax.ShapeDtypeStruct((b, seq_len, D_MODEL), F32),
        scratch_shapes=[pltpu.VMEM((tm + 2 * HALO, PCZ_COLS), F32)],
        compiler_params=_params(2),
        name="merge",
    )(x, mod, g, wm, a, pcz, pcz, pcz, wpool, pscale, wconv, bconv, wa, wp, wc, wo)


def _head_ones():
    blk = np.arange(ATTN_DIM) // HEAD_DIM
    return jnp.asarray(blk[:, None] == blk[None, :], dtype=BF16)


def _layer(x, mod, sw, lw, layer, attend, merge_tile, kv_dtype):
    b, seq_len, _ = x.shape
    t = b * seq_len
    x2 = x.reshape(t, D_MODEL)
    x2 = _ffn(x2, mod, lw['g_ffn1'], sw['w1g'], sw['w1u'], sw['w1d'], layer, 0, seq_len)
    q, k, v, pcz = _proj(x2, mod, lw['g_mix'], sw['w_in'], layer, sw['ones'], lw['g_q'], lw['g_k'],
                         seq_len, kv_dtype)
    a, extras = attend(q, k, v)
    x3 = _merge(x2.reshape(b, seq_len, D_MODEL), mod, lw['g_mix'], sw['w_in'], layer,
                a.reshape(b, seq_len, ATTN_DIM), pcz.reshape(b, seq_len, PCZ_COLS),
                sw['w_pool'], lw['pool_scale'], lw['w_conv'], lw['b_conv'],
                sw['w_br_attn'], sw['w_br_pool'], sw['w_br_conv'], sw['w_out'], merge_tile)
    x2 = _ffn(x3.reshape(t, D_MODEL), mod, lw['g_ffn2'], sw['w2g'], sw['w2u'], sw['w2d'], layer, 6,
              seq_len)
    return x2.reshape(b, seq_len, D_MODEL), k, v, extras


def kernel(x_prompt, x_sample, cache_k, cache_v, c, c_ctx, w_mod, b_mod, g_ffn1, w_ffn1_gate, w_ffn1_up, w_ffn1_down, g_mix, w_in, g_q, g_k, rpb, w_pool, pool_scale, w_conv, b_conv, w_br_attn, w_br_pool, w_br_conv, w_out, g_ffn2, w_ffn2_gate, w_ffn2_up, w_ffn2_down):
    batch, seq, _ = x_prompt.shape
    dec_batch, dec_seq, _ = x_sample.shape
    past = cache_k.shape[3]
    assert dec_batch + 1 <= SUBLANES

    cvec = jnp.zeros((SUBLANES, D_MODEL), F32).at[0].set(c_ctx).at[1:1 + dec_batch].set(c)
    mod = _mod_table(cvec, w_mod, b_mod).reshape(DEPTH, SUBLANES, N_MOD, D_MODEL)

    na_bias = _na_bias_tables(rpb)
    group_eye = jnp.eye(POOL_GROUPS, dtype=F32)
    sw = {
        'w1g': w_ffn1_gate.astype(BF16), 'w1u': w_ffn1_up.astype(BF16), 'w1d': w_ffn1_down.astype(BF16),
        'w2g': w_ffn2_gate.astype(BF16), 'w2u': w_ffn2_up.astype(BF16), 'w2d': w_ffn2_down.astype(BF16),
        'w_in': w_in.astype(BF16), 'ones': _head_ones(),
        'w_pool': (group_eye[None, :, None, :, None] * w_pool[:, :, :, None, :]
                   ).reshape(DEPTH, POOL_DIM, POOL_DIM).astype(BF16),
        'w_br_attn': w_br_attn.astype(BF16), 'w_br_pool': w_br_pool.astype(BF16),
        'w_br_conv': w_br_conv.astype(BF16), 'w_out': w_out.astype(BF16),
    }
    from_heads = lambda z: z.transpose(0, 2, 1, 3).reshape(dec_batch, past, ATTN_DIM).astype(BF16)
    xp, xs = x_prompt, x_sample
    ctx_kv = []
    heads = None
    for l in range(DEPTH):
        row = lambda p: p[l].reshape(1, -1)
        lw = {
            'g_ffn1': row(g_ffn1), 'g_mix': row(g_mix), 'g_ffn2': row(g_ffn2),
            'g_q': jnp.tile(g_q[l], N_HEADS).reshape(1, ATTN_DIM),
            'g_k': jnp.tile(g_k[l], N_HEADS).reshape(1, ATTN_DIM),
            'pool_scale': row(pool_scale), 'w_conv': w_conv[l], 'b_conv': row(b_conv),
        }
        emit_heads = l == DEPTH - 1

        def attend_ctx(q, k, v):
            res = _ctx_attention(q, k, v, seq, prev_kv=tuple(ctx_kv) if emit_heads else ())
            return res[0], tuple(res[1:])

        xp, kp, vp, extras = _layer(xp, mod[l, 0:1], sw, lw, l, attend_ctx, seq, F32)
        ctx_kv.append((kp, vp))
        if emit_heads:
            heads = extras
        kc, vc = from_heads(cache_k[:, l]), from_heads(cache_v[:, l])
        bias = na_bias[l]

        def attend_lat(q, k, v):
            r3 = lambda z: z.reshape(dec_batch, dec_seq, ATTN_DIM)
            a = _na_attention(r3(q), r3(k), r3(v), kc, vc, bias)
            return a.reshape(dec_batch * dec_seq, ATTN_DIM), ()

        xs, _, _, _ = _layer(xs, mod[l, 1:1 + dec_batch], sw, lw, l, attend_lat, MERGE_TILE_LATENT, BF16)
    new_k, new_v = heads
    return (xp, xs, new_k, new_v)
```
